```python
import jax, jax.numpy as jnp
from jax import lax
import numpy as np

D_MODEL = 2048
BATCH = 4
SEQ = 8192
DEPTH = 4

N_MIXERS = 3
N_A = (DEPTH + 2) // 3
N_B = (DEPTH + 1) // 3
N_C = DEPTH // 3
POOL_WINDOWS = (2, 4, 8, 16)
N_POOL_GROUPS = len(POOL_WINDOWS)
POOL_GROUP = D_MODEL // N_POOL_GROUPS
CONV_WIDTH = 3
HEAD_DIM = 128
N_HEADS = D_MODEL // HEAD_DIM
Q_BLOCK = 128
D_FF = -(-8 * D_MODEL // (3 * 256)) * 256
N_MOD = 6
EPS = 1e-6

kernel_name = "hybrid_pool_conv_stickbreak_adaln"


def rmsnorm(x, g):
    xf = x.astype(jnp.float32)
    ms = jnp.mean(xf * xf, axis=-1, keepdims=True)
    return (xf * lax.rsqrt(ms + EPS)).astype(x.dtype) * g


def modulate(h, shift, scale):
    return h * (1 + scale[:, None, :]) + shift[:, None, :]


def causal_window_mean(u, w):
    S = u.shape[1]
    cs = lax.cumsum(u, axis=1)
    cs_shift = jnp.pad(cs, ((0, 0), (w, 0), (0, 0)))[:, :S]
    count = jnp.minimum(jnp.arange(1, S + 1), w).astype(jnp.float32)
    return (cs - cs_shift) / count[None, :, None]


def pool_mixer(h, w_pool, pool_scale):
    B, S, D = h.shape
    hf = h.reshape(B, S, N_POOL_GROUPS, POOL_GROUP).astype(jnp.float32)
    pooled = jnp.stack(
        [causal_window_mean(hf[:, :, g], w) for g, w in enumerate(POOL_WINDOWS)], axis=2)
    diff = (pooled - hf).astype(h.dtype)
    y = jnp.einsum('bsgc,gcd->bsgd', diff, w_pool).reshape(B, S, D)
    return y * pool_scale


def conv_mixer(h, w_in, conv_w, w_out):
    u = h @ w_in
    b_gate, c_gate, v = jnp.split(u, 3, axis=-1)
    z = c_gate * v
    zc = lax.conv_general_dilated(
        z, conv_w, window_strides=(1,), padding=((CONV_WIDTH - 1, 0),),
        dimension_numbers=('NWC', 'WIO', 'NWC'), feature_group_count=z.shape[-1])
    return (b_gate * zc) @ w_out


def stick_breaking_attention(q, k, v):
    B, H, S, Dh = q.shape
    nb = S // Q_BLOCK
    qb = q.reshape(B, H, nb, Q_BLOCK, Dh).transpose(2, 0, 1, 3, 4)
    key_pos = jnp.arange(S)
    scale = Dh ** -0.5

    def block(args):
        q_blk, i = args
        z = jnp.einsum('bhqd,bhkd->bhqk', q_blk, k,
                       preferred_element_type=jnp.float32) * scale
        q_pos = i * Q_BLOCK + jnp.arange(Q_BLOCK)
        causal = key_pos[None, :] < q_pos[:, None]
        log_beta = jax.nn.log_sigmoid(z)
        log_1m = jnp.where(causal, jax.nn.log_sigmoid(-z), 0.0)
        suffix = lax.cumsum(log_1m, axis=3, reverse=True) - log_1m
        a = jnp.where(causal, jnp.exp(log_beta + suffix), 0.0)
        return jnp.einsum('bhqk,bhkd->bhqd', a.astype(v.dtype), v)

    o = lax.map(block, (qb, jnp.arange(nb)))
    return o.transpose(1, 2, 0, 3, 4).reshape(B, H, S, Dh)


def sb_mixer(h, w_qkv, w_o):
    B, S, D = h.shape
    qkv = (h @ w_qkv).reshape(B, S, 3, N_HEADS, HEAD_DIM)
    q = qkv[:, :, 0].transpose(0, 2, 1, 3)
    k = qkv[:, :, 1].transpose(0, 2, 1, 3)
    v = qkv[:, :, 2].transpose(0, 2, 1, 3)
    o = stick_breaking_attention(q, k, v)
    return o.transpose(0, 2, 1, 3).reshape(B, S, D) @ w_o


def swiglu(h, w_gate, w_up, w_down):
    return (jax.nn.silu(h @ w_gate) * (h @ w_up)) @ w_down


def setup_inputs(seed: int = 0) -> dict:
    key = jax.random.key(seed)
    ks = jax.random.split(key, 20)
    D = D_MODEL
    nrm = jax.random.normal
    f32 = jnp.float32
    return {
        "x": nrm(ks[0], (BATCH, SEQ, D), f32),
        "c": nrm(ks[1], (BATCH, D), f32),
        "norm_mix_g": 1.0 + 0.05 * nrm(ks[2], (DEPTH, D), f32),
        "norm_ffn_g": 1.0 + 0.05 * nrm(ks[3], (DEPTH, D), f32),
        "w_mod": 0.5 * D ** -0.5 * nrm(ks[4], (DEPTH, D, N_MOD * D), f32),
        "b_mod": 0.1 * nrm(ks[5], (DEPTH, N_MOD * D), f32),
        "pool_w": POOL_GROUP ** -0.5 * nrm(ks[6], (N_A, N_POOL_GROUPS, POOL_GROUP, POOL_GROUP), f32),
        "pool_scale": 1.0 + 0.1 * nrm(ks[7], (N_A, D), f32),
        "conv_w_in": D ** -0.5 * nrm(ks[8], (N_B, D, 3 * D), f32),
        "conv_w": CONV_WIDTH ** -0.5 * nrm(ks[9], (N_B, CONV_WIDTH, 1, D), f32),
        "conv_w_out": D ** -0.5 * nrm(ks[10], (N_B, D, D), f32),
        "sb_w_qkv": D ** -0.5 * nrm(ks[11], (N_C, D, 3 * D), f32),
        "sb_w_o": D ** -0.5 * nrm(ks[12], (N_C, D, D), f32),
        "ffn_w_gate": D ** -0.5 * nrm(ks[13], (DEPTH, D, D_FF), f32),
        "ffn_w_up": D ** -0.5 * nrm(ks[14], (DEPTH, D, D_FF), f32),
        "ffn_w_down": D_FF ** -0.5 * nrm(ks[15], (DEPTH, D_FF, D), f32),
        "final_g": 1.0 + 0.05 * nrm(ks[16], (D,), f32),
    }


def reference(x, c, norm_mix_g, norm_ffn_g, w_mod, b_mod, pool_w, pool_scale,
              conv_w_in, conv_w, conv_w_out, sb_w_qkv, sb_w_o,
              ffn_w_gate, ffn_w_up, ffn_w_down, final_g):
    mod = jnp.einsum('bd,lde->lbe', jax.nn.silu(c), w_mod) + b_mod[:, None, :]
    h = x
    for i in range(DEPTH):
        shift_m, scale_m, gate_m, shift_f, scale_f, gate_f = jnp.split(mod[i], N_MOD, axis=-1)
        u = modulate(rmsnorm(h, norm_mix_g[i]), shift_m, scale_m)
        kind, j = i % N_MIXERS, i // N_MIXERS
        if kind == 0:
            y = pool_mixer(u, pool_w[j], pool_scale[j])
        elif kind == 1:
            y = conv_mixer(u, conv_w_in[j], conv_w[j], conv_w_out[j])
        else:
            y = sb_mixer(u, sb_w_qkv[j], sb_w_o[j])
        h = h + gate_m[:, None, :] * y
        u = modulate(rmsnorm(h, norm_ffn_g[i]), shift_f, scale_f)
        h = h + gate_f[:, None, :] * swiglu(u, ffn_w_gate[i], ffn_w_up[i], ffn_w_down[i])
    return rmsnorm(h, final_g)
```

```python
import functools
import math

import jax
import jax.numpy as jnp
from jax import lax
from jax.experimental import pallas as pl
from jax.experimental.pallas import tpu as pltpu

POOL_WINDOWS = (2, 4, 8, 16)
HEAD_DIM = 128
CONV_WIDTH = 3
N_MIXERS = 3
N_MOD = 6
EPS = 1e-6

HALO = 16

ROW_TILE = 512
FF_TILE = 512
QKV_COL_TILE = 1024
OUT_COL_TILE = 1024
MOD_COL_TILE = 1024
ATTN_Q_TILE = 256
ATTN_K_TILE = 256

V7X_VMEM_BYTES = 64 * 1024 * 1024
VMEM_LIMIT_BYTES = 3 * V7X_VMEM_BYTES // 4

F32 = jnp.float32
BF16 = jnp.bfloat16
LOG2E = math.log2(math.e)


def _tile(n, pref, align):
    if n <= pref:
        return n
    t = (pref // align) * align
    while t >= align:
        if n % t == 0:
            return t
        t -= align
    raise ValueError(f"no {align}-aligned tile <= {pref} divides {n}")


def _params(semantics):
    return pltpu.CompilerParams(dimension_semantics=semantics, vmem_limit_bytes=VMEM_LIMIT_BYTES)


def _rmsnorm(x, g):
    ms = jnp.mean(x * x, axis=-1, keepdims=True)
    return x * lax.rsqrt(ms + EPS) * g


def _norm_mod(x, g, shift, scale):
    return _rmsnorm(x, g) * (1.0 + scale) + shift


def _dot(a, b):
    return jnp.dot(a, b, preferred_element_type=F32)


def _mod_kernel(c_ref, w_ref, b_ref, o_ref):
    c = c_ref[...]
    sc = c / (1.0 + jnp.exp(-c))
    o_ref[0] = _dot(sc.astype(BF16), w_ref[0].astype(BF16)) + b_ref[0]


def _modulation(c, w_mod, b_mod):
    depth, d, e = w_mod.shape
    b = c.shape[0]
    tn = _tile(e, MOD_COL_TILE, 128)
    return pl.pallas_call(
        _mod_kernel,
        grid=(depth, e // tn),
        in_specs=[
            pl.BlockSpec((b, d), lambda l, j: (0, 0)),
            pl.BlockSpec((1, d, tn), lambda l, j: (l, 0, j)),
            pl.BlockSpec((1, 1, tn), lambda l, j: (l, 0, j)),
        ],
        out_specs=pl.BlockSpec((1, b, tn), lambda l, j: (l, 0, j)),
        out_shape=jax.ShapeDtypeStruct((depth, b, e), F32),
        compiler_params=_params(("parallel", "parallel")),
        name="adaln_mod",
    )(c, w_mod, b_mod.reshape(depth, 1, e))


def _pool_kernel(h_ref, halo_ref, mod_ref, g_ref, w_ref, ps_ref, o_ref, u_ref, *, tm, seq, windows):
    i = pl.program_id(0)
    pos0 = (i * tm) % seq
    shift, scale, gate = mod_ref[0:1, :], mod_ref[1:2, :], mod_ref[2:3, :]
    g = g_ref[...]
    u_halo = _norm_mod(halo_ref[...], g, shift, scale)
    u_ref[0:HALO, :] = jnp.where(pos0 == 0, 0.0, u_halo)
    u_ref[HALO:, :] = _norm_mod(h_ref[...], g, shift, scale)
    pos = pos0 + lax.broadcasted_iota(jnp.int32, (tm, 1), 0)
    c = w_ref.shape[1]
    for grp, w in enumerate(windows):
        cols = slice(grp * c, (grp + 1) * c)
        cur = u_ref[pl.ds(HALO, tm), cols]
        acc = cur
        for k in range(1, w):
            acc = acc + u_ref[pl.ds(HALO - k, tm), cols]
        count = jnp.minimum(pos + 1, w).astype(F32)
        diff = acc / count - cur
        y = _dot(diff.astype(BF16), w_ref[grp]) * ps_ref[:, cols]
        o_ref[:, cols] = h_ref[:, cols] + gate[:, cols] * y


def _pool_layer(h, mod, layer, gain, w_pool, pool_scale, seq):
    t, d = h.shape
    groups, c, _ = w_pool.shape
    assert groups == len(POOL_WINDOWS) and groups * c == d and max(POOL_WINDOWS) <= HALO
    tm = _tile(seq, ROW_TILE, HALO)
    halo_blocks = tm // HALO
    kern = functools.partial(_pool_kernel, tm=tm, seq=seq, windows=POOL_WINDOWS)
    return pl.pallas_call(
        kern,
        grid=(t // tm,),
        in_specs=[
            pl.BlockSpec((tm, d), lambda i: (i, 0)),
            pl.BlockSpec((HALO, d), lambda i: (jnp.maximum(i * halo_blocks - 1, 0), 0)),
            pl.BlockSpec((None, None, N_MOD, d), lambda i: (layer, (i * tm) // seq, 0, 0)),
            pl.BlockSpec((1, d), lambda i: (0, 0)),
            pl.BlockSpec((groups, c, c), lambda i: (0, 0, 0)),
            pl.BlockSpec((1, d), lambda i: (0, 0)),
        ],
        out_specs=pl.BlockSpec((tm, d), lambda i: (i, 0)),
        out_shape=jax.ShapeDtypeStruct((t, d), F32),
        scratch_shapes=[pltpu.VMEM((tm + HALO, d), F32)],
        compiler_params=_params(("parallel",)),
        name=f"pool_mixer_{layer}",
    )(h, h, mod, gain.reshape(1, d), w_pool.astype(BF16), pool_scale.reshape(1, d))


def _conv_kernel(h_ref, halo_ref, mod_ref, g_ref, wb_ref, wc_ref, wv_ref, cw_ref, wo_ref, o_ref,
                 u_ref, z_ref, *, tm, seq):
    i, j = pl.program_id(0), pl.program_id(1)
    last = pl.num_programs(1) - 1

    @pl.when(j == 0)
    def _():
        pos0 = (i * tm) % seq
        shift, scale = mod_ref[0:1, :], mod_ref[1:2, :]
        g = g_ref[...]
        u_halo = _norm_mod(halo_ref[...], g, shift, scale)
        u_ref[0:HALO, :] = jnp.where(pos0 == 0, 0.0, u_halo).astype(BF16)
        u_ref[HALO:, :] = _norm_mod(h_ref[...], g, shift, scale).astype(BF16)

    u_all = u_ref[...]
    z_ref[...] = _dot(u_all, wc_ref[...]) * _dot(u_all, wv_ref[...])
    b_gate = _dot(u_ref[pl.ds(HALO, tm), :], wb_ref[...])
    zc = cw_ref[0:1, :] * z_ref[pl.ds(HALO - 2, tm), :]
    zc = zc + cw_ref[1:2, :] * z_ref[pl.ds(HALO - 1, tm), :]
    zc = zc + cw_ref[2:3, :] * z_ref[pl.ds(HALO, tm), :]
    contrib = _dot((b_gate * zc).astype(BF16), wo_ref[...])

    @pl.when(j == 0)
    def _():
        o_ref[...] = contrib

    @pl.when(j > 0)
    def _():
        o_ref[...] += contrib

    @pl.when(j == last)
    def _():
        o_ref[...] = h_ref[...] + mod_ref[2:3, :] * o_ref[...]


def _conv_layer(h, mod, layer, gain, w_in, conv_w, w_out, seq):
    t, d = h.shape
    assert conv_w.shape == (CONV_WIDTH, 1, d) and CONV_WIDTH - 1 <= HALO
    tm = _tile(seq, ROW_TILE, HALO)
    tn = _tile(d, FF_TILE, 128)
    nj = d // tn
    halo_blocks = tm // HALO
    w_in = w_in.astype(BF16)
    kern = functools.partial(_conv_kernel, tm=tm, seq=seq)
    return pl.pallas_call(
        kern,
        grid=(t // tm, nj),
        in_specs=[
            pl.BlockSpec((tm, d), lambda i, j: (i, 0)),
            pl.BlockSpec((HALO, d), lambda i, j: (jnp.maximum(i * halo_blocks - 1, 0), 0)),
            pl.BlockSpec((None, None, N_MOD, d), lambda i, j: (layer, (i * tm) // seq, 0, 0)),
            pl.BlockSpec((1, d), lambda i, j: (0, 0)),
            pl.BlockSpec((d, tn), lambda i, j: (0, j)),
            pl.BlockSpec((d, tn), lambda i, j: (0, nj + j)),
            pl.BlockSpec((d, tn), lambda i, j: (0, 2 * nj + j)),
            pl.BlockSpec((CONV_WIDTH, tn), lambda i, j: (0, j)),
            pl.BlockSpec((tn, d), lambda i, j: (j, 0)),
        ],
        out_specs=pl.BlockSpec((tm, d), lambda i, j: (i, 0)),
        out_shape=jax.ShapeDtypeStruct((t, d), F32),
        scratch_shapes=[pltpu.VMEM((tm + HALO, d), BF16), pltpu.VMEM((tm + HALO, tn), F32)],
        compiler_params=_params(("parallel", "arbitrary")),
        name=f"conv_mixer_{layer}",
    )(h, h, mod, gain.reshape(1, d), w_in, w_in, w_in, conv_w.reshape(CONV_WIDTH, d), w_out.astype(BF16))


def _qkv_kernel(h_ref, mod_ref, g_ref, w_ref, o_ref, u_ref, *, q_tiles, q_scale):
    j = pl.program_id(1)

    @pl.when(j == 0)
    def _():
        u_ref[...] = _norm_mod(h_ref[...], g_ref[...], mod_ref[0:1, :], mod_ref[1:2, :]).astype(BF16)

    scale = jnp.where(j < q_tiles, q_scale, 1.0).astype(F32)
    o_ref[...] = (_dot(u_ref[...], w_ref[...]) * scale).astype(BF16)


def _qkv_proj(h, mod, layer, gain, w_qkv, seq):
    t, d = h.shape
    e = w_qkv.shape[1]
    tm = _tile(seq, ROW_TILE, HALO)
    tn = _tile(d, QKV_COL_TILE, 128)
    kern = functools.partial(_qkv_kernel, q_tiles=d // tn, q_scale=HEAD_DIM ** -0.5 * LOG2E)
    return pl.pallas_call(
        kern,
        grid=(t // tm, e // tn),
        in_specs=[
            pl.BlockSpec((tm, d), lambda i, j: (i, 0)),
            pl.BlockSpec((None, None, N_MOD, d), lambda i, j: (layer, (i * tm) // seq, 0, 0)),
            pl.BlockSpec((1, d), lambda i, j: (0, 0)),
            pl.BlockSpec((d, tn), lambda i, j: (0, j)),
        ],
        out_specs=pl.BlockSpec((tm, tn), lambda i, j: (i, j)),
        out_shape=jax.ShapeDtypeStruct((t, e), BF16),
        scratch_shapes=[pltpu.VMEM((tm, d), BF16)],
        compiler_params=_params(("parallel", "arbitrary")),
        name=f"sb_qkv_{layer}",
    )(h, mod, gain.reshape(1, d), w_qkv.astype(BF16))


def _sb_block(q, k, v, tri, carry, acc, causal_mask):
    z = lax.dot_general(q, k, (((1,), (1,)), ((), ())), preferred_element_type=F32)
    soft = jnp.log2(1.0 + jnp.exp2(-jnp.abs(z)))
    log_beta = jnp.minimum(z, 0.0) - soft
    log_1m = log_beta - z
    if causal_mask is not None:
        log_1m = jnp.where(causal_mask, log_1m, 0.0)
    hi = log_1m.astype(BF16)
    lo = (log_1m - hi.astype(F32)).astype(BF16)
    suffix = _dot(hi, tri) + _dot(lo, tri)
    a = jnp.exp2(log_beta + suffix + carry)
    if causal_mask is not None:
        a = jnp.where(causal_mask, a, 0.0)
    acc = acc + _dot(a.astype(BF16), v)
    carry = carry + jnp.sum(log_1m, axis=1, keepdims=True)
    return carry, acc


def _sb_attn_kernel(q_ref, k_ref, v_ref, o_ref, *, tq, tk):
    qi = pl.program_id(2)
    q = q_ref[...]
    row = lax.broadcasted_iota(jnp.int32, (tk, tk), 0)
    col = lax.broadcasted_iota(jnp.int32, (tk, tk), 1)
    tri = (row > col).astype(BF16)
    kd = pl.multiple_of(qi * tq, tq)
    mask = lax.broadcasted_iota(jnp.int32, (tq, tk), 1) < lax.broadcasted_iota(jnp.int32, (tq, tk), 0)
    carry0 = jnp.zeros((tq, 1), F32)
    acc0 = jnp.zeros((tq, HEAD_DIM), F32)
    carry, acc = _sb_block(q, k_ref[pl.ds(kd, tk), :], v_ref[pl.ds(kd, tk), :], tri, carry0, acc0, mask)

    def body(it, state):
        ks = pl.multiple_of((qi - 1 - it) * tk, tk)
        return _sb_block(q, k_ref[pl.ds(ks, tk), :], v_ref[pl.ds(ks, tk), :], tri, *state, None)

    carry, acc = lax.fori_loop(0, qi, body, (carry, acc))
    o_ref[...] = acc.astype(BF16)


def _sb_attention(qkv, batch, seq, d):
    t = qkv.shape[0]
    heads = d // HEAD_DIM
    tq = _tile(seq, ATTN_Q_TILE, 128)
    tk = tq
    nq = seq // tq
    kern = functools.partial(_sb_attn_kernel, tq=tq, tk=tk)
    return pl.pallas_call(
        kern,
        grid=(batch, heads, nq),
        in_specs=[
            pl.BlockSpec((tq, HEAD_DIM), lambda b, hd, qi: (b * nq + qi, hd)),
            pl.BlockSpec((seq, HEAD_DIM), lambda b, hd, qi: (b, heads + hd)),
            pl.BlockSpec((seq, HEAD_DIM), lambda b, hd, qi: (b, 2 * heads + hd)),
        ],
        out_specs=pl.BlockSpec((tq, HEAD_DIM), lambda b, hd, qi: (b * nq + qi, hd)),
        out_shape=jax.ShapeDtypeStruct((t, d), BF16),
        compiler_params=_params(("parallel", "parallel", "arbitrary")),
        name="sb_attention",
    )(qkv, qkv, qkv)


def _oproj_kernel(a_ref, w_ref, h_ref, mod_ref, o_ref):
    o_ref[...] = h_ref[...] + mod_ref[2:3, :] * _dot(a_ref[...], w_ref[...])


def _out_proj(attn, h, mod, layer, w_o, seq):
    t, d = h.shape
    tm = _tile(seq, ROW_TILE, HALO)
    tn = _tile(d, OUT_COL_TILE, 128)
    return pl.pallas_call(
        _oproj_kernel,
        grid=(t // tm, d // tn),
        in_specs=[
            pl.BlockSpec((tm, d), lambda i, j: (i, 0)),
            pl.BlockSpec((d, tn), lambda i, j: (0, j)),
            pl.BlockSpec((tm, tn), lambda i, j: (i, j)),
            pl.BlockSpec((None, None, N_MOD, tn), lambda i, j: (layer, (i * tm) // seq, 0, j)),
        ],
        out_specs=pl.BlockSpec((tm, tn), lambda i, j: (i, j)),
        out_shape=jax.ShapeDtypeStruct((t, d), F32),
        compiler_params=_params(("parallel", "arbitrary")),
        name=f"sb_out_proj_{layer}",
    )(attn, w_o.astype(BF16), h, mod)


def _ffn_kernel(h_ref, mod_ref, g_ref, wg_ref, wu_ref, wd_ref, fg_ref, o_ref, u_ref, *, final_norm):
    f = pl.program_id(1)
    last = pl.num_programs(1) - 1

    @pl.when(f == 0)
    def _():
        u_ref[...] = _norm_mod(h_ref[...], g_ref[...], mod_ref[3:4, :], mod_ref[4:5, :]).astype(BF16)

    u = u_ref[...]
    gate = _dot(u, wg_ref[...])
    up = _dot(u, wu_ref[...])
    act = (gate / (1.0 + jnp.exp(-gate)) * up).astype(BF16)
    contrib = _dot(act, wd_ref[...])

    @pl.when(f == 0)
    def _():
        o_ref[...] = contrib

    @pl.when(f > 0)
    def _():
        o_ref[...] += contrib

    @pl.when(f == last)
    def _():
        out = h_ref[...] + mod_ref[5:6, :] * o_ref[...]
        if final_norm:
            out = _rmsnorm(out, fg_ref[...])
        o_ref[...] = out


def _ffn_layer(h, mod, layer, gain, w_gate, w_up, w_down, final_g, seq, final_norm):
    t, d = h.shape
    ff = w_gate.shape[1]
    tm = _tile(seq, ROW_TILE, HALO)
    tf = _tile(ff, FF_TILE, 128)
    kern = functools.partial(_ffn_kernel, final_norm=final_norm)
    return pl.pallas_call(
        kern,
        grid=(t // tm, ff // tf),
        in_specs=[
            pl.BlockSpec((tm, d), lambda i, f: (i, 0)),
            pl.BlockSpec((None, None, N_MOD, d), lambda i, f: (layer, (i * tm) // seq, 0, 0)),
            pl.BlockSpec((1, d), lambda i, f: (0, 0)),
            pl.BlockSpec((d, tf), lambda i, f: (0, f)),
            pl.BlockSpec((d, tf), lambda i, f: (0, f)),
            pl.BlockSpec((tf, d), lambda i, f: (f, 0)),
            pl.BlockSpec((1, d), lambda i, f: (0, 0)),
        ],
        out_specs=pl.BlockSpec((tm, d), lambda i, f: (i, 0)),
        out_shape=jax.ShapeDtypeStruct((t, d), F32),
        scratch_shapes=[pltpu.VMEM((tm, d), BF16)],
        compiler_params=_params(("parallel", "arbitrary")),
        name=f"ffn_{layer}",
    )(h, mod, gain.reshape(1, d), w_gate.astype(BF16), w_up.astype(BF16), w_down.astype(BF16),
      final_g.reshape(1, d))


def kernel(x, c, norm_mix_g, norm_ffn_g, w_mod, b_mod, pool_w, pool_scale, conv_w_in, conv_w,
           conv_w_out, sb_w_qkv, sb_w_o, ffn_w_gate, ffn_w_up, ffn_w_down, final_g):
    batch, seq, d = x.shape
    depth = w_mod.shape[0]
    assert w_mod.shape == (depth, d, N_MOD * d) and d % HEAD_DIM == 0
    mod = _modulation(c, w_mod, b_mod).reshape(depth, batch, N_MOD, d)
    h = x.reshape(batch * seq, d)
    for i in range(depth):
        kind, j = i % N_MIXERS, i // N_MIXERS
        if kind == 0:
            h = _pool_layer(h, mod, i, norm_mix_g[i], pool_w[j], pool_scale[j], seq)
        elif kind == 1:
            h = _conv_layer(h, mod, i, norm_mix_g[i], conv_w_in[j], conv_w[j], conv_w_out[j], seq)
        else:
            qkv = _qkv_proj(h, mod, i, norm_mix_g[i], sb_w_qkv[j], seq)
            attn = _sb_attention(qkv, batch, seq, d)
            h = _out_proj(attn, h, mod, i, sb_w_o[j], seq)
        h = _ffn_layer(h, mod, i, norm_ffn_g[i], ffn_w_gate[i], ffn_w_up[i], ffn_w_down[i],
                       final_g, seq, final_norm=(i == depth - 1))
    return h.reshape(batch, seq, d)
```

```python
import functools
import math

import jax
import jax.numpy as jnp
from jax import lax
from jax.experimental import pallas as pl
from jax.experimental.pallas import tpu as pltpu

POOL_WINDOWS = (2, 4, 8, 16)
HEAD_DIM = 128
CONV_WIDTH = 3
N_MIXERS = 3
N_MOD = 6
EPS = 1e-6

HALO = 16

ROW_TILE = 512
PROJ_ROW_TILE = 1024
FF_TILE = 512
QKV_COL_TILE = 1024
OUT_COL_TILE = 1024
MOD_COL_TILE = 1024
ACC_COL_TILE = 512
ATTN_TILE = 256
ATTN_CHAINS = 4

SB_DEAD_LOG2 = -151.0
SB_PARKED_LOG2 = -1e30

V7X_VMEM_BYTES = 64 * 1024 * 1024
VMEM_LIMIT_BYTES = 7 * V7X_VMEM_BYTES // 8

F32 = jnp.float32
BF16 = jnp.bfloat16
LOG2E = math.log2(math.e)


def _tile(n, pref, align):
    if n <= pref:
        return n
    t = (pref // align) * align
    while t >= align:
        if n % t == 0:
            return t
        t -= align
    raise ValueError(f"no {align}-aligned tile <= {pref} divides {n}")


def _params(semantics):
    return pltpu.CompilerParams(dimension_semantics=semantics, vmem_limit_bytes=VMEM_LIMIT_BYTES)


def _rmsnorm(x, g):
    ms = jnp.mean(x * x, axis=-1, keepdims=True)
    return x * lax.rsqrt(ms + EPS) * g


def _norm_mod(x, g, shift, scale):
    return _rmsnorm(x, g) * (1.0 + scale) + shift


def _norm_mod_ref(x_ref, g, shift, scale):
    x = x_ref[...]
    inv = lax.rsqrt(jnp.mean(x * x, axis=-1, keepdims=True) + EPS)
    return x_ref[...] * inv * (g * (1.0 + scale)) + shift


def _dot(a, b):
    return jnp.dot(a, b, preferred_element_type=F32)


def _accumulate_dot(o_ref, a, w_ref):
    n = o_ref.shape[1]
    tn = _tile(n, ACC_COL_TILE, 128)
    for c0 in range(0, n, tn):
        o_ref[:, c0:c0 + tn] += _dot(a, w_ref[:, c0:c0 + tn])


def _mod_kernel(c_ref, w_ref, b_ref, o_ref):
    c = c_ref[...]
    sc = c / (1.0 + jnp.exp(-c))
    o_ref[0] = _dot(sc.astype(BF16), w_ref[0].astype(BF16)) + b_ref[0]


def _modulation(c, w_mod, b_mod):
    depth, d, e = w_mod.shape
    b = c.shape[0]
    tn = _tile(e, MOD_COL_TILE, 128)
    return pl.pallas_call(
        _mod_kernel,
        grid=(depth, e // tn),
        in_specs=[
            pl.BlockSpec((b, d), lambda l, j: (0, 0)),
            pl.BlockSpec((1, d, tn), lambda l, j: (l, 0, j)),
            pl.BlockSpec((1, 1, tn), lambda l, j: (l, 0, j)),
        ],
        out_specs=pl.BlockSpec((1, b, tn), lambda l, j: (l, 0, j)),
        out_shape=jax.ShapeDtypeStruct((depth, b, e), F32),
        compiler_params=_params(("parallel", "parallel")),
        name="adaln_mod",
    )(c, w_mod, b_mod.reshape(depth, 1, e))


def _pool_kernel(h_ref, halo_ref, mod_ref, g_ref, w_ref, ps_ref, o_ref, u_ref, *, tm, seq, windows):
    i = pl.program_id(0)
    pos0 = (i * tm) % seq
    shift, scale, gate = mod_ref[0:1, :], mod_ref[1:2, :], mod_ref[2:3, :]
    g = g_ref[...]
    u_halo = _norm_mod(halo_ref[...], g, shift, scale)
    u_ref[0:HALO, :] = jnp.where(pos0 == 0, 0.0, u_halo)
    u_ref[HALO:, :] = _norm_mod_ref(h_ref, g, shift, scale)
    pos = pos0 + lax.broadcasted_iota(jnp.int32, (tm, 1), 0)
    c = w_ref.shape[1]
    for grp, w in enumerate(windows):
        cols = slice(grp * c, (grp + 1) * c)
        cur = u_ref[pl.ds(HALO, tm), cols]
        acc = cur
        for k in range(1, w):
            acc = acc + u_ref[pl.ds(HALO - k, tm), cols]
        count = jnp.minimum(pos + 1, w).astype(F32)
        diff = acc / count - cur
        y = _dot(diff.astype(BF16), w_ref[grp]) * ps_ref[:, cols]
        o_ref[:, cols] = h_ref[:, cols] + gate[:, cols] * y


def _pool_layer(h, mod, layer, gain, w_pool, pool_scale, seq):
    t, d = h.shape
    groups, c, _ = w_pool.shape
    assert groups == len(POOL_WINDOWS) and groups * c == d and max(POOL_WINDOWS) <= HALO
    tm = _tile(seq, ROW_TILE, HALO)
    halo_blocks = tm // HALO
    kern = functools.partial(_pool_kernel, tm=tm, seq=seq, windows=POOL_WINDOWS)
    return pl.pallas_call(
        kern,
        grid=(t // tm,),
        in_specs=[
            pl.BlockSpec((tm, d), lambda i: (i, 0)),
            pl.BlockSpec((HALO, d), lambda i: (jnp.maximum(i * halo_blocks - 1, 0), 0)),
            pl.BlockSpec((None, None, N_MOD, d), lambda i: (layer, (i * tm) // seq, 0, 0)),
            pl.BlockSpec((1, d), lambda i: (0, 0)),
            pl.BlockSpec((groups, c, c), lambda i: (0, 0, 0)),
            pl.BlockSpec((1, d), lambda i: (0, 0)),
        ],
        out_specs=pl.BlockSpec((tm, d), lambda i: (i, 0)),
        out_shape=jax.ShapeDtypeStruct((t, d), F32),
        scratch_shapes=[pltpu.VMEM((tm + HALO, d), F32)],
        compiler_params=_params(("parallel",)),
        name=f"pool_mixer_{layer}",
    )(h, h, mod, gain.reshape(1, d), w_pool.astype(BF16), pool_scale.reshape(1, d))


def _conv_kernel(h_ref, halo_ref, mod_ref, g_ref, wb_ref, wc_ref, wv_ref, cw_ref, wo_ref, o_ref,
                 u_ref, z_ref, *, tm, seq):
    i, j = pl.program_id(0), pl.program_id(1)
    last = pl.num_programs(1) - 1

    @pl.when(j == 0)
    def _():
        pos0 = (i * tm) % seq
        shift, scale = mod_ref[0:1, :], mod_ref[1:2, :]
        g = g_ref[...]
        u_halo = _norm_mod(halo_ref[...], g, shift, scale)
        u_ref[0:HALO, :] = jnp.where(pos0 == 0, 0.0, u_halo).astype(BF16)
        u_ref[HALO:, :] = _norm_mod_ref(h_ref, g, shift, scale).astype(BF16)
        o_ref[...] = jnp.zeros_like(o_ref)

    u_all = u_ref[...]
    z_ref[...] = _dot(u_all, wc_ref[...]) * _dot(u_all, wv_ref[...])
    b_gate = _dot(u_ref[pl.ds(HALO, tm), :], wb_ref[...])
    zc = cw_ref[0:1, :] * z_ref[pl.ds(HALO - 2, tm), :]
    zc = zc + cw_ref[1:2, :] * z_ref[pl.ds(HALO - 1, tm), :]
    zc = zc + cw_ref[2:3, :] * z_ref[pl.ds(HALO, tm), :]
    _accumulate_dot(o_ref, (b_gate * zc).astype(BF16), wo_ref)

    @pl.when(j == last)
    def _():
        o_ref[...] = h_ref[...] + mod_ref[2:3, :] * o_ref[...]


def _conv_layer(h, mod, layer, gain, w_in, conv_w, w_out, seq):
    t, d = h.shape
    assert conv_w.shape == (CONV_WIDTH, 1, d) and CONV_WIDTH - 1 <= HALO
    tm = _tile(seq, ROW_TILE, HALO)
    tn = _tile(d, FF_TILE, 128)
    nj = d // tn
    halo_blocks = tm // HALO
    w_in = w_in.astype(BF16)
    kern = functools.partial(_conv_kernel, tm=tm, seq=seq)
    return pl.pallas_call(
        kern,
        grid=(t // tm, nj),
        in_specs=[
            pl.BlockSpec((tm, d), lambda i, j: (i, 0)),
            pl.BlockSpec((HALO, d), lambda i, j: (jnp.maximum(i * halo_blocks - 1, 0), 0)),
            pl.BlockSpec((None, None, N_MOD, d), lambda i, j: (layer, (i * tm) // seq, 0, 0)),
            pl.BlockSpec((1, d), lambda i, j: (0, 0)),
            pl.BlockSpec((d, tn), lambda i, j: (0, j)),
            pl.BlockSpec((d, tn), lambda i, j: (0, nj + j)),
            pl.BlockSpec((d, tn), lambda i, j: (0, 2 * nj + j)),
            pl.BlockSpec((CONV_WIDTH, tn), lambda i, j: (0, j)),
            pl.BlockSpec((tn, d), lambda i, j: (j, 0)),
        ],
        out_specs=pl.BlockSpec((tm, d), lambda i, j: (i, 0)),
        out_shape=jax.ShapeDtypeStruct((t, d), F32),
        scratch_shapes=[pltpu.VMEM((tm + HALO, d), BF16), pltpu.VMEM((tm + HALO, tn), F32)],
        compiler_params=_params(("parallel", "arbitrary")),
        name=f"conv_mixer_{layer}",
    )(h, h, mod, gain.reshape(1, d), w_in, w_in, w_in, conv_w.reshape(CONV_WIDTH, d), w_out.astype(BF16))


def _qkv_kernel(h_ref, mod_ref, g_ref, w_ref, o_ref, u_ref, *, q_tiles, q_scale):
    j = pl.program_id(1)

    @pl.when(j == 0)
    def _():
        u_ref[...] = _norm_mod_ref(h_ref, g_ref[...], mod_ref[0:1, :], mod_ref[1:2, :]).astype(BF16)

    scale = jnp.where(j < q_tiles, q_scale, 1.0).astype(F32)
    o_ref[...] = (_dot(u_ref[...], w_ref[...]) * scale).astype(BF16)


def _qkv_proj(h, mod, layer, gain, w_qkv, seq):
    t, d = h.shape
    e = w_qkv.shape[1]
    tm = _tile(seq, PROJ_ROW_TILE, HALO)
    tn = _tile(d, QKV_COL_TILE, 128)
    kern = functools.partial(_qkv_kernel, q_tiles=d // tn, q_scale=HEAD_DIM ** -0.5 * LOG2E)
    return pl.pallas_call(
        kern,
        grid=(t // tm, e // tn),
        in_specs=[
            pl.BlockSpec((tm, d), lambda i, j: (i, 0)),
            pl.BlockSpec((None, None, N_MOD, d), lambda i, j: (layer, (i * tm) // seq, 0, 0)),
            pl.BlockSpec((1, d), lambda i, j: (0, 0)),
            pl.BlockSpec((d, tn), lambda i, j: (0, j)),
        ],
        out_specs=pl.BlockSpec((tm, tn), lambda i, j: (i, j)),
        out_shape=jax.ShapeDtypeStruct((t, e), BF16),
        scratch_shapes=[pltpu.VMEM((tm, d), BF16)],
        compiler_params=_params(("parallel", "arbitrary")),
        name=f"sb_qkv_{layer}",
    )(h, mod, gain.reshape(1, d), w_qkv.astype(BF16))


def _sb_tiles(q_ref, k_ref, v_ref, key_starts, carries, tri, causal_mask, ts):
    chains = range(len(carries))
    zs = [lax.dot_general(q_ref[pl.ds(r * ts, ts), :], k_ref[pl.ds(key_starts[r], ts), :],
                          (((1,), (1,)), ((), ())), preferred_element_type=F32) for r in chains]
    log_betas, log_1ms, sums = [], [], []
    for r in chains:
        z = zs[r]
        log_beta = jnp.minimum(z, 0.0) - jnp.log2(1.0 + jnp.exp2(-jnp.abs(z)))
        log_1m = log_beta - z
        if causal_mask is not None:
            log_1m = jnp.where(causal_mask, log_1m, 0.0)
        hi = log_1m.astype(BF16)
        lo = (log_1m - hi.astype(F32)).astype(BF16)
        sums.append(_dot(jnp.concatenate([hi, lo], axis=0), tri))
        log_betas.append(log_beta)
        log_1ms.append(log_1m)
    out = []
    for r in chains:
        a = jnp.exp2(log_betas[r] + (sums[r][:ts] + sums[r][ts:]) + carries[r])
        if causal_mask is not None:
            a = jnp.where(causal_mask, a, 0.0)
        pv = _dot(a.astype(BF16), v_ref[pl.ds(key_starts[r], ts), :])
        out.append((carries[r] + jnp.sum(log_1ms[r], axis=1, keepdims=True), pv))
    return out


def _sb_attn_kernel(q_ref, k_ref, v_ref, o_ref, acc_ref, *, ts, chains):
    base = pl.program_id(2) * chains
    row = lax.broadcasted_iota(jnp.int32, (ts, ts), 0)
    col = lax.broadcasted_iota(jnp.int32, (ts, ts), 1)
    tri = (row > col).astype(BF16)

    def alive(carries):
        return functools.reduce(jnp.maximum, [jnp.max(cr) for cr in carries]) > SB_DEAD_LOG2

    starts = [pl.multiple_of((base + r) * ts, ts) for r in range(chains)]
    zero = jnp.zeros((ts, 1), F32)
    carries = []
    for r, (carry, pv) in enumerate(_sb_tiles(q_ref, k_ref, v_ref, starts, [zero] * chains, tri, col < row, ts)):
        acc_ref[pl.ds(r * ts, ts), :] = pv
        carries.append(carry)

    def cond(state):
        it, live = state[0], state[1]
        return jnp.logical_and(it < base + chains, live)

    def body(state):
        it, carries = state[0], state[2:]
        kbs = [base + r - it for r in range(chains)]
        starts = [pl.multiple_of(jnp.maximum(kb, 0) * ts, ts) for kb in kbs]
        carries = [jnp.where(kb >= 0, carry, SB_PARKED_LOG2) for kb, carry in zip(kbs, carries)]
        new = []
        for r, (carry, pv) in enumerate(_sb_tiles(q_ref, k_ref, v_ref, starts, carries, tri, None, ts)):
            acc_ref[pl.ds(r * ts, ts), :] += pv
            new.append(carry)
        return (it + 1, alive(new), *new)

    lax.while_loop(cond, body, (jnp.int32(1), alive(carries), *carries))
    o_ref[...] = acc_ref[...].astype(BF16)


def _sb_attention(qkv, batch, seq, d):
    t = qkv.shape[0]
    heads = d // HEAD_DIM
    ts = _tile(seq, ATTN_TILE, 128)
    chains = math.gcd(seq // ts, ATTN_CHAINS)
    tq = ts * chains
    nq = seq // tq
    kern = functools.partial(_sb_attn_kernel, ts=ts, chains=chains)
    return pl.pallas_call(
        kern,
        grid=(batch, heads, nq),
        in_specs=[
            pl.BlockSpec((tq, HEAD_DIM), lambda b, hd, qi: (b * nq + qi, hd)),
            pl.BlockSpec((seq, HEAD_DIM), lambda b, hd, qi: (b, heads + hd)),
            pl.BlockSpec((seq, HEAD_DIM), lambda b, hd, qi: (b, 2 * heads + hd)),
        ],
        out_specs=pl.BlockSpec((tq, HEAD_DIM), lambda b, hd, qi: (b * nq + qi, hd)),
        out_shape=jax.ShapeDtypeStruct((t, d), BF16),
        scratch_shapes=[pltpu.VMEM((tq, HEAD_DIM), F32)],
        compiler_params=_params(("parallel", "parallel", "arbitrary")),
        name="sb_attention",
    )(qkv, qkv, qkv)


def _oproj_kernel(a_ref, w_ref, h_ref, mod_ref, o_ref):
    o_ref[...] = h_ref[...] + mod_ref[2:3, :] * _dot(a_ref[...], w_ref[...])


def _out_proj(attn, h, mod, layer, w_o, seq):
    t, d = h.shape
    tm = _tile(seq, PROJ_ROW_TILE, HALO)
    tn = _tile(d, OUT_COL_TILE, 128)
    return pl.pallas_call(
        _oproj_kernel,
        grid=(t // tm, d // tn),
        in_specs=[
            pl.BlockSpec((tm, d), lambda i, j: (i, 0)),
            pl.BlockSpec((d, tn), lambda i, j: (0, j)),
            pl.BlockSpec((tm, tn), lambda i, j: (i, j)),
            pl.BlockSpec((None, None, N_MOD, tn), lambda i, j: (layer, (i * tm) // seq, 0, j)),
        ],
        out_specs=pl.BlockSpec((tm, tn), lambda i, j: (i, j)),
        out_shape=jax.ShapeDtypeStruct((t, d), F32),
        compiler_params=_params(("parallel", "arbitrary")),
        name=f"sb_out_proj_{layer}",
    )(attn, w_o.astype(BF16), h, mod)


def _ffn_kernel(h_ref, mod_ref, g_ref, wg_ref, wu_ref, wd_ref, fg_ref, o_ref, u_ref, *, final_norm):
    f = pl.program_id(1)
    last = pl.num_programs(1) - 1

    @pl.when(f == 0)
    def _():
        u_ref[...] = _norm_mod_ref(h_ref, g_ref[...], mod_ref[3:4, :], mod_ref[4:5, :]).astype(BF16)
        o_ref[...] = jnp.zeros_like(o_ref)

    u = u_ref[...]
    gate = _dot(u, wg_ref[...])
    up = _dot(u, wu_ref[...])
    act = (gate / (1.0 + jnp.exp(-gate)) * up).astype(BF16)
    _accumulate_dot(o_ref, act, wd_ref)

    @pl.when(f == last)
    def _():
        out = h_ref[...] + mod_ref[5:6, :] * o_ref[...]
        if final_norm:
            out = _rmsnorm(out, fg_ref[...])
        o_ref[...] = out


def _ffn_layer(h, mod, layer, gain, w_gate, w_up, w_down, final_g, seq, final_norm):
    t, d = h.shape
    ff = w_gate.shape[1]
    tm = _tile(seq, ROW_TILE, HALO)
    tf = _tile(ff, FF_TILE, 128)
    kern = functools.partial(_ffn_kernel, final_norm=final_norm)
    return pl.pallas_call(
        kern,
        grid=(t // tm, ff // tf),
        in_specs=[
            pl.BlockSpec((tm, d), lambda i, f: (i, 0)),
            pl.BlockSpec((None, None, N_MOD, d), lambda i, f: (layer, (i * tm) // seq, 0, 0)),
            pl.BlockSpec((1, d), lambda i, f: (0, 0)),
            pl.BlockSpec((d, tf), lambda i, f: (0, f)),
            pl.BlockSpec((d, tf), lambda i, f: (0, f)),
            pl.BlockSpec((tf, d), lambda i, f: (f, 0)),
            pl.BlockSpec((1, d), lambda i, f: (0, 0)),
        ],
        out_specs=pl.BlockSpec((tm, d), lambda i, f: (i, 0)),
        out_shape=jax.ShapeDtypeStruct((t, d), F32),
        scratch_shapes=[pltpu.VMEM((tm, d), BF16)],
        compiler_params=_params(("parallel", "arbitrary")),
        name=f"ffn_{layer}",
    )(h, mod, gain.reshape(1, d), w_gate.astype(BF16), w_up.astype(BF16), w_down.astype(BF16),
      final_g.reshape(1, d))


def kernel(x, c, norm_mix_g, norm_ffn_g, w_mod, b_mod, pool_w, pool_scale, conv_w_in, conv_w,
           conv_w_out, sb_w_qkv, sb_w_o, ffn_w_gate, ffn_w_up, ffn_w_down, final_g):
    batch, seq, d = x.shape
    depth = w_mod.shape[0]
    assert w_mod.shape == (depth, d, N_MOD * d) and d % HEAD_DIM == 0
    mod = _modulation(c, w_mod, b_mod).reshape(depth, batch, N_MOD, d)
    h = x.reshape(batch * seq, d)
    for i in range(depth):
        kind, j = i % N_MIXERS, i // N_MIXERS
        if kind == 0:
            h = _pool_layer(h, mod, i, norm_mix_g[i], pool_w[j], pool_scale[j], seq)
        elif kind == 1:
            h = _conv_layer(h, mod, i, norm_mix_g[i], conv_w_in[j], conv_w[j], conv_w_out[j], seq)
        else:
            qkv = _qkv_proj(h, mod, i, norm_mix_g[i], sb_w_qkv[j], seq)
            attn = _sb_attention(qkv, batch, seq, d)
            h = _out_proj(attn, h, mod, i, sb_w_o[j], seq)
        h = _ffn_layer(h, mod, i, norm_ffn_g[i], ffn_w_gate[i], ffn_w_up[i], ffn_w_down[i],
                       final_g, seq, final_norm=(i == depth - 1))
    return h.reshape(batch, seq, d)
```

```python
import functools
import math

import jax
import jax.numpy as jnp
from jax import lax
from jax.experimental import pallas as pl
from jax.experimental.pallas import tpu as pltpu

POOL_WINDOWS = (2, 4, 8, 16)
HEAD_DIM = 128
CONV_WIDTH = 3
N_MIXERS = 3
N_MOD = 6
EPS = 1e-6

HALO = 16

ROW_TILE = 512
PROJ_ROW_TILE = 1024
FFN_ROW_TILE = 1024
FF_TILE = 512
QKV_COL_TILE = 1024
OUT_COL_TILE = 1024
MOD_COL_TILE = 1024
ACC_COL_TILE = 512
ATTN_TILE = 256
ATTN_CHAINS = 8

SB_DEAD_LOG2 = -151.0
SB_PARKED_LOG2 = -1e30

V7X_VMEM_BYTES = 64 * 1024 * 1024
VMEM_LIMIT_BYTES = 7 * V7X_VMEM_BYTES // 8

F32 = jnp.float32
BF16 = jnp.bfloat16
LOG2E = math.log2(math.e)


def _tile(n, pref, align):
    if n <= pref:
        return n
    t = (pref // align) * align
    while t >= align:
        if n % t == 0:
            return t
        t -= align
    raise ValueError(f"no {align}-aligned tile <= {pref} divides {n}")


def _params(semantics):
    return pltpu.CompilerParams(dimension_semantics=semantics, vmem_limit_bytes=VMEM_LIMIT_BYTES)


def _rmsnorm(x, g):
    ms = jnp.mean(x * x, axis=-1, keepdims=True)
    return x * lax.rsqrt(ms + EPS) * g


def _norm_mod(x, g, shift, scale):
    return _rmsnorm(x, g) * (1.0 + scale) + shift


def _norm_mod_ref(x_ref, g, shift, scale):
    x = x_ref[...]
    inv = lax.rsqrt(jnp.mean(x * x, axis=-1, keepdims=True) + EPS)
    return x_ref[...] * inv * (g * (1.0 + scale)) + shift


def _dot(a, b):
    return jnp.dot(a, b, preferred_element_type=F32)


def _accumulate_dot(o_ref, gate, a, w_ref):
    n = o_ref.shape[1]
    tn = _tile(n, ACC_COL_TILE, 128)
    for c0 in range(0, n, tn):
        o_ref[:, c0:c0 + tn] += gate[:, c0:c0 + tn] * _dot(a, w_ref[:, c0:c0 + tn])


def _start_from_residual(o_ref, h_ref, step):
    @pl.when(step < 1)
    def _():
        o_ref[...] = h_ref[...]


def _mod_kernel(c_ref, w_ref, b_ref, o_ref):
    c = c_ref[...]
    sc = c / (1.0 + jnp.exp(-c))
    o_ref[0] = _dot(sc.astype(BF16), w_ref[0].astype(BF16)) + b_ref[0]


def _modulation(c, w_mod, b_mod):
    depth, d, e = w_mod.shape
    b = c.shape[0]
    tn = _tile(e, MOD_COL_TILE, 128)
    return pl.pallas_call(
        _mod_kernel,
        grid=(depth, e // tn),
        in_specs=[
            pl.BlockSpec((b, d), lambda l, j: (0, 0)),
            pl.BlockSpec((1, d, tn), lambda l, j: (l, 0, j)),
            pl.BlockSpec((1, 1, tn), lambda l, j: (l, 0, j)),
        ],
        out_specs=pl.BlockSpec((1, b, tn), lambda l, j: (l, 0, j)),
        out_shape=jax.ShapeDtypeStruct((depth, b, e), F32),
        compiler_params=_params(("parallel", "parallel")),
        name="adaln_mod",
    )(c, w_mod, b_mod.reshape(depth, 1, e))


def _pool_kernel(h_ref, halo_ref, mod_ref, g_ref, w_ref, ps_ref, o_ref, u_ref, *, tm, seq, windows):
    i = pl.program_id(0)
    pos0 = (i * tm) % seq
    shift, scale, gate = mod_ref[0:1, :], mod_ref[1:2, :], mod_ref[2:3, :]
    g = g_ref[...]
    u_halo = _norm_mod(halo_ref[...], g, shift, scale)
    u_ref[0:HALO, :] = jnp.where(pos0 == 0, 0.0, u_halo)
    u_ref[HALO:, :] = _norm_mod_ref(h_ref, g, shift, scale)
    pos = pos0 + lax.broadcasted_iota(jnp.int32, (tm, 1), 0)
    c = w_ref.shape[1]
    for grp, w in enumerate(windows):
        cols = slice(grp * c, (grp + 1) * c)
        cur = u_ref[pl.ds(HALO, tm), cols]
        acc = cur
        for k in range(1, w):
            acc = acc + u_ref[pl.ds(HALO - k, tm), cols]
        count = jnp.minimum(pos + 1, w).astype(F32)
        diff = acc / count - cur
        y = _dot(diff.astype(BF16), w_ref[grp]) * ps_ref[:, cols]
        o_ref[:, cols] = h_ref[:, cols] + gate[:, cols] * y


def _pool_layer(h, mod, layer, gain, w_pool, pool_scale, seq):
    t, d = h.shape
    groups, c, _ = w_pool.shape
    assert groups == len(POOL_WINDOWS) and groups * c == d and max(POOL_WINDOWS) <= HALO
    tm = _tile(seq, ROW_TILE, HALO)
    halo_blocks = tm // HALO
    kern = functools.partial(_pool_kernel, tm=tm, seq=seq, windows=POOL_WINDOWS)
    return pl.pallas_call(
        kern,
        grid=(t // tm,),
        in_specs=[
            pl.BlockSpec((tm, d), lambda i: (i, 0)),
            pl.BlockSpec((HALO, d), lambda i: (jnp.maximum(i * halo_blocks - 1, 0), 0)),
            pl.BlockSpec((None, None, N_MOD, d), lambda i: (layer, (i * tm) // seq, 0, 0)),
            pl.BlockSpec((1, d), lambda i: (0, 0)),
            pl.BlockSpec((groups, c, c), lambda i: (0, 0, 0)),
            pl.BlockSpec((1, d), lambda i: (0, 0)),
        ],
        out_specs=pl.BlockSpec((tm, d), lambda i: (i, 0)),
        out_shape=jax.ShapeDtypeStruct((t, d), F32),
        scratch_shapes=[pltpu.VMEM((tm + HALO, d), F32)],
        compiler_params=_params(("parallel",)),
        name=f"pool_mixer_{layer}",
    )(h, h, mod, gain.reshape(1, d), w_pool.astype(BF16), pool_scale.reshape(1, d))


def _conv_kernel(h_ref, halo_ref, mod_ref, g_ref, wb_ref, wc_ref, wv_ref, cw_ref, wo_ref, o_ref,
                 u_ref, z_ref, *, tm, seq):
    i, j = pl.program_id(0), pl.program_id(1)
    last = pl.num_programs(1) - 1

    @pl.when(j == 0)
    def _():
        pos0 = (i * tm) % seq
        shift, scale = mod_ref[0:1, :], mod_ref[1:2, :]
        g = g_ref[...]
        u_halo = _norm_mod(halo_ref[...], g, shift, scale)
        u_ref[0:HALO, :] = jnp.where(pos0 == 0, 0.0, u_halo).astype(BF16)
        u_ref[HALO:, :] = _norm_mod_ref(h_ref, g, shift, scale).astype(BF16)

    _start_from_residual(o_ref, h_ref, j)
    u_all = u_ref[...]
    z_ref[...] = _dot(u_all, wc_ref[...]) * _dot(u_all, wv_ref[...])
    b_gate = _dot(u_ref[pl.ds(HALO, tm), :], wb_ref[...])
    zc = cw_ref[0:1, :] * z_ref[pl.ds(HALO - 2, tm), :]
    zc = zc + cw_ref[1:2, :] * z_ref[pl.ds(HALO - 1, tm), :]
    zc = zc + cw_ref[2:3, :] * z_ref[pl.ds(HALO, tm), :]
    _accumulate_dot(o_ref, mod_ref[2:3, :], (b_gate * zc).astype(BF16), wo_ref)


def _conv_layer(h, mod, layer, gain, w_in, conv_w, w_out, seq):
    t, d = h.shape
    assert conv_w.shape == (CONV_WIDTH, 1, d) and CONV_WIDTH - 1 <= HALO
    tm = _tile(seq, ROW_TILE, HALO)
    tn = _tile(d, FF_TILE, 128)
    nj = d // tn
    halo_blocks = tm // HALO
    w_in = w_in.astype(BF16)
    kern = functools.partial(_conv_kernel, tm=tm, seq=seq)
    return pl.pallas_call(
        kern,
        grid=(t // tm, nj),
        in_specs=[
            pl.BlockSpec((tm, d), lambda i, j: (i, 0)),
            pl.BlockSpec((HALO, d), lambda i, j: (jnp.maximum(i * halo_blocks - 1, 0), 0)),
            pl.BlockSpec((None, None, N_MOD, d), lambda i, j: (layer, (i * tm) // seq, 0, 0)),
            pl.BlockSpec((1, d), lambda i, j: (0, 0)),
            pl.BlockSpec((d, tn), lambda i, j: (0, j)),
            pl.BlockSpec((d, tn), lambda i, j: (0, nj + j)),
            pl.BlockSpec((d, tn), lambda i, j: (0, 2 * nj + j)),
            pl.BlockSpec((CONV_WIDTH, tn), lambda i, j: (0, j)),
            pl.BlockSpec((tn, d), lambda i, j: (j, 0)),
        ],
        out_specs=pl.BlockSpec((tm, d), lambda i, j: (i, 0)),
        out_shape=jax.ShapeDtypeStruct((t, d), F32),
        scratch_shapes=[pltpu.VMEM((tm + HALO, d), BF16), pltpu.VMEM((tm + HALO, tn), F32)],
        compiler_params=_params(("parallel", "arbitrary")),
        name=f"conv_mixer_{layer}",
    )(h, h, mod, gain.reshape(1, d), w_in, w_in, w_in, conv_w.reshape(CONV_WIDTH, d), w_out.astype(BF16))


def _qkv_kernel(h_ref, mod_ref, g_ref, w_ref, o_ref, u_ref, *, q_tiles, q_scale):
    j = pl.program_id(1)

    @pl.when(j == 0)
    def _():
        u_ref[...] = _norm_mod_ref(h_ref, g_ref[...], mod_ref[0:1, :], mod_ref[1:2, :]).astype(BF16)

    scale = jnp.where(j < q_tiles, q_scale, 1.0).astype(F32)
    o_ref[...] = (_dot(u_ref[...], w_ref[...]) * scale).astype(BF16)


def _qkv_proj(h, mod, layer, gain, w_qkv, seq):
    t, d = h.shape
    e = w_qkv.shape[1]
    tm = _tile(seq, PROJ_ROW_TILE, HALO)
    tn = _tile(d, QKV_COL_TILE, 128)
    kern = functools.partial(_qkv_kernel, q_tiles=d // tn, q_scale=HEAD_DIM ** -0.5 * LOG2E)
    return pl.pallas_call(
        kern,
        grid=(t // tm, e // tn),
        in_specs=[
            pl.BlockSpec((tm, d), lambda i, j: (i, 0)),
            pl.BlockSpec((None, None, N_MOD, d), lambda i, j: (layer, (i * tm) // seq, 0, 0)),
            pl.BlockSpec((1, d), lambda i, j: (0, 0)),
            pl.BlockSpec((d, tn), lambda i, j: (0, j)),
        ],
        out_specs=pl.BlockSpec((tm, tn), lambda i, j: (i, j)),
        out_shape=jax.ShapeDtypeStruct((t, e), BF16),
        scratch_shapes=[pltpu.VMEM((tm, d), BF16)],
        compiler_params=_params(("parallel", "arbitrary")),
        name=f"sb_qkv_{layer}",
    )(h, mod, gain.reshape(1, d), w_qkv.astype(BF16))


def _sb_tiles(q_ref, k_ref, v_ref, jobs, carries, tri, causal_mask, ts):
    starts = [pl.multiple_of(jnp.maximum(kb, 0) * ts, ts) for _, kb, _ in jobs]
    zs = [lax.dot_general(q_ref[pl.ds(r * ts, ts), :], k_ref[pl.ds(ks, ts), :],
                          (((1,), (1,)), ((), ())), preferred_element_type=F32)
          for (r, _, _), ks in zip(jobs, starts)]
    log_betas, row_sums, sums = [], [], []
    for (_, _, diagonal), z in zip(jobs, zs):
        neg_abs = pltpu.bitcast(pltpu.bitcast(z, jnp.uint32) | jnp.uint32(1 << 31), F32)
        log_beta = jnp.minimum(z, 0.0) - jnp.log2(1.0 + jnp.exp2(neg_abs))
        log_1m = log_beta - z
        if diagonal:
            log_1m = jnp.where(causal_mask, log_1m, 0.0)
        hi = log_1m.astype(BF16)
        lo = (log_1m - hi.astype(F32)).astype(BF16)
        sums.append(_dot(jnp.concatenate([hi, lo], axis=1), tri))
        log_betas.append(log_beta)
        row_sums.append(jnp.sum(log_1m, axis=1, keepdims=True))
    carries = list(carries)
    pvs = [None] * len(carries)
    for j, (r, kb, diagonal) in enumerate(jobs):
        carry = jnp.where(kb >= 0, carries[r], SB_PARKED_LOG2)
        a = jnp.exp2(log_betas[j] + sums[j] + carry)
        if diagonal:
            a = jnp.where(causal_mask, a, 0.0)
        pv = _dot(a.astype(BF16), v_ref[pl.ds(starts[j], ts), :])
        pvs[r] = pv if pvs[r] is None else pvs[r] + pv
        carries[r] = carry + row_sums[j]
    return carries, pvs


def _sb_attn_kernel(q_ref, k_ref, v_ref, o_ref, acc_ref, *, ts, chains):
    base = pl.program_id(2) * chains
    row = lax.broadcasted_iota(jnp.int32, (ts, ts), 0)
    col = lax.broadcasted_iota(jnp.int32, (ts, ts), 1)
    tri = jnp.tile((row > col).astype(BF16), (2, 1))

    def alive(carries):
        return functools.reduce(jnp.maximum, [jnp.max(cr) for cr in carries]) > SB_DEAD_LOG2

    jobs = [(r, base + r, True) for r in range(chains)] + [(r, base + r - 1, False) for r in range(chains)]
    carries, pvs = _sb_tiles(q_ref, k_ref, v_ref, jobs, [jnp.zeros((ts, 1), F32)] * chains, tri, col < row, ts)
    for r in range(chains):
        acc_ref[pl.ds(r * ts, ts), :] = pvs[r]

    def cond(state):
        it, live = state[0], state[1]
        return jnp.logical_and(it < base + chains, live)

    def body(state):
        it, carries = state[0], state[2:]
        jobs = [(r, base + r - it, False) for r in range(chains)]
        carries, pvs = _sb_tiles(q_ref, k_ref, v_ref, jobs, carries, tri, None, ts)
        for r in range(chains):
            acc_ref[pl.ds(r * ts, ts), :] += pvs[r]
        return (it + 1, alive(carries), *carries)

    lax.while_loop(cond, body, (jnp.int32(2), alive(carries), *carries))
    o_ref[...] = acc_ref[...].astype(BF16)


def _sb_attention(qkv, batch, seq, d):
    t = qkv.shape[0]
    heads = d // HEAD_DIM
    ts = _tile(seq, ATTN_TILE, 128)
    chains = math.gcd(seq // ts, ATTN_CHAINS)
    tq = ts * chains
    nq = seq // tq
    kern = functools.partial(_sb_attn_kernel, ts=ts, chains=chains)
    return pl.pallas_call(
        kern,
        grid=(batch, heads, nq),
        in_specs=[
            pl.BlockSpec((tq, HEAD_DIM), lambda b, hd, qi: (b * nq + qi, hd)),
            pl.BlockSpec((seq, HEAD_DIM), lambda b, hd, qi: (b, heads + hd)),
            pl.BlockSpec((seq, HEAD_DIM), lambda b, hd, qi: (b, 2 * heads + hd)),
        ],
        out_specs=pl.BlockSpec((tq, HEAD_DIM), lambda b, hd, qi: (b * nq + qi, hd)),
        out_shape=jax.ShapeDtypeStruct((t, d), BF16),
        scratch_shapes=[pltpu.VMEM((tq, HEAD_DIM), F32)],
        compiler_params=_params(("parallel", "parallel", "arbitrary")),
        name="sb_attention",
    )(qkv, qkv, qkv)


def _oproj_kernel(a_ref, w_ref, h_ref, mod_ref, o_ref):
    o_ref[...] = h_ref[...] + mod_ref[2:3, :] * _dot(a_ref[...], w_ref[...])


def _out_proj(attn, h, mod, layer, w_o, seq):
    t, d = h.shape
    tm = _tile(seq, PROJ_ROW_TILE, HALO)
    tn = _tile(d, OUT_COL_TILE, 128)
    return pl.pallas_call(
        _oproj_kernel,
        grid=(t // tm, d // tn),
        in_specs=[
            pl.BlockSpec((tm, d), lambda i, j: (i, 0)),
            pl.BlockSpec((d, tn), lambda i, j: (0, j)),
            pl.BlockSpec((tm, tn), lambda i, j: (i, j)),
            pl.BlockSpec((None, None, N_MOD, tn), lambda i, j: (layer, (i * tm) // seq, 0, j)),
        ],
        out_specs=pl.BlockSpec((tm, tn), lambda i, j: (i, j)),
        out_shape=jax.ShapeDtypeStruct((t, d), F32),
        compiler_params=_params(("parallel", "arbitrary")),
        name=f"sb_out_proj_{layer}",
    )(attn, w_o.astype(BF16), h, mod)


def _ffn_kernel(h_hbm, mod_ref, g_ref, wg_ref, wu_ref, wd_ref, fg_ref, o_ref, u_ref, h_ref, h_sem, *,
                final_norm):
    i, f = pl.program_id(0), pl.program_id(1)
    nb, nf = pl.num_programs(0), pl.num_programs(1)
    last = nf - 1
    tm = h_ref.shape[0]

    def h_copy(block):
        return pltpu.make_async_copy(h_hbm.at[pl.ds(pl.multiple_of(block * tm, tm), tm), :], h_ref, h_sem)

    @pl.when(f == 0)
    def _():
        @pl.when(i == 0)
        def _():
            h_copy(0).start()

        h_copy(i).wait()
        u_ref[...] = _norm_mod_ref(h_ref, g_ref[...], mod_ref[3:4, :], mod_ref[4:5, :]).astype(BF16)

    _start_from_residual(o_ref, h_ref, f)

    refill = jnp.minimum(1, last)

    @pl.when(jnp.logical_and(f == refill, i + 1 < nb))
    def _():
        h_copy(i + 1).start()

    u = u_ref[...]
    gate = _dot(u, wg_ref[...])
    up = _dot(u, wu_ref[...])
    act = (gate / (1.0 + jnp.exp(-gate)) * up).astype(BF16)
    _accumulate_dot(o_ref, mod_ref[5:6, :], act, wd_ref)

    if final_norm:
        @pl.when(f == last)
        def _():
            o_ref[...] = _rmsnorm(o_ref[...], fg_ref[...])


def _ffn_layer(h, mod, layer, gain, w_gate, w_up, w_down, final_g, seq, final_norm):
    t, d = h.shape
    ff = w_gate.shape[1]
    tm = _tile(seq, FFN_ROW_TILE, HALO)
    tf = _tile(ff, FF_TILE, 128)
    kern = functools.partial(_ffn_kernel, final_norm=final_norm)
    return pl.pallas_call(
        kern,
        grid=(t // tm, ff // tf),
        in_specs=[
            pl.BlockSpec(memory_space=pl.ANY),
            pl.BlockSpec((None, None, N_MOD, d), lambda i, f: (layer, (i * tm) // seq, 0, 0)),
            pl.BlockSpec((1, d), lambda i, f: (0, 0)),
            pl.BlockSpec((d, tf), lambda i, f: (0, f)),
            pl.BlockSpec((d, tf), lambda i, f: (0, f)),
            pl.BlockSpec((tf, d), lambda i, f: (f, 0)),
            pl.BlockSpec((1, d), lambda i, f: (0, 0)),
        ],
        out_specs=pl.BlockSpec((tm, d), lambda i, f: (i, 0)),
        out_shape=jax.ShapeDtypeStruct((t, d), F32),
        scratch_shapes=[pltpu.VMEM((tm, d), BF16), pltpu.VMEM((tm, d), F32), pltpu.SemaphoreType.DMA(())],
        compiler_params=_params(("arbitrary", "arbitrary")),
        name=f"ffn_{layer}",
    )(h, mod, gain.reshape(1, d), w_gate.astype(BF16), w_up.astype(BF16), w_down.astype(BF16),
      final_g.reshape(1, d))


def kernel(x, c, norm_mix_g, norm_ffn_g, w_mod, b_mod, pool_w, pool_scale, conv_w_in, conv_w,
           conv_w_out, sb_w_qkv, sb_w_o, ffn_w_gate, ffn_w_up, ffn_w_down, final_g):
    batch, seq, d = x.shape
    depth = w_mod.shape[0]
    assert w_mod.shape == (depth, d, N_MOD * d) and d % HEAD_DIM == 0
    mod = _modulation(c, w_mod, b_mod).reshape(depth, batch, N_MOD, d)
    h = x.reshape(batch * seq, d)
    for i in range(depth):
        kind, j = i % N_MIXERS, i // N_MIXERS
        if kind == 0:
            h = _pool_layer(h, mod, i, norm_mix_g[i], pool_w[j], pool_scale[j], seq)
        elif kind == 1:
            h = _conv_layer(h, mod, i, norm_mix_g[i], conv_w_in[j], conv_w[j], conv_w_out[j], seq)
        else:
            qkv = _qkv_proj(h, mod, i, norm_mix_g[i], sb_w_qkv[j], seq)
            attn = _sb_attention(qkv, batch, seq, d)
            h = _out_proj(attn, h, mod, i, sb_w_o[j], seq)
        h = _ffn_layer(h, mod, i, norm_ffn_g[i], ffn_w_gate[i], ffn_w_up[i], ffn_w_down[i],
                       final_g, seq, final_norm=(i == depth - 1))
    return h.reshape(batch, seq, d)
```

```python
import functools
import math

import jax
import jax.numpy as jnp
from jax import lax
from jax.experimental import pallas as pl
from jax.experimental.pallas import tpu as pltpu

POOL_WINDOWS = (2, 4, 8, 16)
HEAD_DIM = 128
CONV_WIDTH = 3
N_MIXERS = 3
N_MOD = 6
EPS = 1e-6

HALO = 16
SUBLANES = 8
POOL_HALO = 32

ROW_TILE = 512
PROJ_ROW_TILE = 1024
FFN_ROW_TILE = 1024
FF_TILE = 512
QKV_COL_TILE = 1024
OUT_COL_TILE = 1024
MOD_COL_TILE = 1024
ACC_COL_TILE = 512
ATTN_TILE = 256
ATTN_CHAINS = 8

SB_DEAD_LOG2 = -151.0
SB_PARKED_LOG2 = -1e30

V7X_VMEM_BYTES = 64 * 1024 * 1024
VMEM_LIMIT_BYTES = 7 * V7X_VMEM_BYTES // 8

F32 = jnp.float32
BF16 = jnp.bfloat16
LOG2E = math.log2(math.e)


def _tile(n, pref, align):
    if n <= pref:
        return n
    t = (pref // align) * align
    while t >= align:
        if n % t == 0:
            return t
        t -= align
    raise ValueError(f"no {align}-aligned tile <= {pref} divides {n}")


def _params(semantics):
    return pltpu.CompilerParams(dimension_semantics=semantics, vmem_limit_bytes=VMEM_LIMIT_BYTES)


def _rmsnorm(x, g):
    ms = jnp.mean(x * x, axis=-1, keepdims=True)
    return x * lax.rsqrt(ms + EPS) * g


def _norm_mod(x, g, shift, scale):
    return _rmsnorm(x, g) * (1.0 + scale) + shift


def _norm_mod_ref(x_ref, g, shift, scale):
    x = x_ref[...]
    inv = lax.rsqrt(jnp.mean(x * x, axis=-1, keepdims=True) + EPS)
    return x_ref[...] * inv * (g * (1.0 + scale)) + shift


def _dot(a, b):
    return jnp.dot(a, b, preferred_element_type=F32)


def _accumulate_dot(o_ref, gate, a, w_ref):
    n = o_ref.shape[1]
    tn = _tile(n, ACC_COL_TILE, 128)
    for c0 in range(0, n, tn):
        o_ref[:, c0:c0 + tn] += gate[:, c0:c0 + tn] * _dot(a, w_ref[:, c0:c0 + tn])


def _start_from_residual(o_ref, h_ref, step):
    @pl.when(step < 1)
    def _():
        o_ref[...] = h_ref[...]


def _mod_kernel(c_ref, w_ref, b_ref, o_ref):
    c = c_ref[...]
    sc = c / (1.0 + jnp.exp(-c))
    o_ref[0] = _dot(sc.astype(BF16), w_ref[0].astype(BF16)) + b_ref[0]


def _modulation(c, w_mod, b_mod):
    depth, d, e = w_mod.shape
    b = c.shape[0]
    tn = _tile(e, MOD_COL_TILE, 128)
    return pl.pallas_call(
        _mod_kernel,
        grid=(depth, e // tn),
        in_specs=[
            pl.BlockSpec((b, d), lambda l, j: (0, 0)),
            pl.BlockSpec((1, d, tn), lambda l, j: (l, 0, j)),
            pl.BlockSpec((1, 1, tn), lambda l, j: (l, 0, j)),
        ],
        out_specs=pl.BlockSpec((1, b, tn), lambda l, j: (l, 0, j)),
        out_shape=jax.ShapeDtypeStruct((depth, b, e), F32),
        compiler_params=_params(("parallel", "parallel")),
        name="adaln_mod",
    )(c, w_mod, b_mod.reshape(depth, 1, e))


def _pool_kernel(h_ref, halo_ref, mod_ref, g_ref, w_ref, ps_ref, o_ref, u_ref, a_ref, b_ref, *, tm, seq, windows):
    i = pl.program_id(0)
    pos0 = (i * tm) % seq
    shift, scale, gate = mod_ref[0:1, :], mod_ref[1:2, :], mod_ref[2:3, :]
    g = g_ref[...]
    u_halo = _norm_mod(halo_ref[...], g, shift, scale)
    u_ref[0:POOL_HALO, :] = jnp.where(pos0 == 0, 0.0, u_halo)
    u_ref[POOL_HALO:, :] = _norm_mod_ref(h_ref, g, shift, scale)
    pos = pos0 + lax.broadcasted_iota(jnp.int32, (tm, 1), 0)
    c = w_ref.shape[1]
    for grp, w in enumerate(windows):
        cols = slice(grp * c, (grp + 1) * c)
        src, span, lo = u_ref, 1, 0
        for dst in (a_ref, b_ref, a_ref, b_ref):
            if span >= w:
                break
            lo += SUBLANES
            n = tm + POOL_HALO - lo
            dst[pl.ds(lo, n), cols] = src[pl.ds(lo, n), cols] + src[pl.ds(lo - span, n), cols]
            src, span = dst, 2 * span
        cur = u_ref[pl.ds(POOL_HALO, tm), cols]
        count = jnp.minimum(pos + 1, w).astype(F32)
        diff = src[pl.ds(POOL_HALO, tm), cols] / count - cur
        y = _dot(diff.astype(BF16), w_ref[grp]) * ps_ref[:, cols]
        o_ref[:, cols] = h_ref[:, cols] + gate[:, cols] * y


def _pool_layer(h, mod, layer, gain, w_pool, pool_scale, seq):
    t, d = h.shape
    groups, c, _ = w_pool.shape
    passes = max(POOL_WINDOWS).bit_length() - 1
    assert groups == len(POOL_WINDOWS) and groups * c == d
    assert all(w & (w - 1) == 0 for w in POOL_WINDOWS) and passes <= 4 and passes * SUBLANES <= POOL_HALO
    tm = _tile(seq, ROW_TILE, POOL_HALO)
    halo_blocks = tm // POOL_HALO
    kern = functools.partial(_pool_kernel, tm=tm, seq=seq, windows=POOL_WINDOWS)
    rows = tm + POOL_HALO
    return pl.pallas_call(
        kern,
        grid=(t // tm,),
        in_specs=[
            pl.BlockSpec((tm, d), lambda i: (i, 0)),
            pl.BlockSpec((POOL_HALO, d), lambda i: (jnp.maximum(i * halo_blocks - 1, 0), 0)),
            pl.BlockSpec((None, None, N_MOD, d), lambda i: (layer, (i * tm) // seq, 0, 0)),
            pl.BlockSpec((1, d), lambda i: (0, 0)),
            pl.BlockSpec((groups, c, c), lambda i: (0, 0, 0)),
            pl.BlockSpec((1, d), lambda i: (0, 0)),
        ],
        out_specs=pl.BlockSpec((tm, d), lambda i: (i, 0)),
        out_shape=jax.ShapeDtypeStruct((t, d), F32),
        scratch_shapes=[pltpu.VMEM((rows, d), F32), pltpu.VMEM((rows, d), F32), pltpu.VMEM((rows, d), F32)],
        compiler_params=_params(("parallel",)),
        name=f"pool_mixer_{layer}",
    )(h, h, mod, gain.reshape(1, d), w_pool.astype(BF16), pool_scale.reshape(1, d))


def _conv_kernel(h_ref, halo_ref, mod_ref, g_ref, wb_ref, wc_ref, wv_ref, cw_ref, wo_ref, o_ref,
                 u_ref, z_ref, *, tm, seq):
    i, j = pl.program_id(0), pl.program_id(1)

    @pl.when(j == 0)
    def _():
        pos0 = (i * tm) % seq
        shift, scale = mod_ref[0:1, :], mod_ref[1:2, :]
        g = g_ref[...]
        u_halo = _norm_mod(halo_ref[...], g, shift, scale)
        u_ref[0:HALO, :] = jnp.where(pos0 == 0, 0.0, u_halo).astype(BF16)
        u_ref[HALO:, :] = _norm_mod_ref(h_ref, g, shift, scale).astype(BF16)

    _start_from_residual(o_ref, h_ref, j)
    u_all = u_ref[...]
    z_ref[...] = _dot(u_all, wc_ref[...]) * _dot(u_all, wv_ref[...])
    b_gate = _dot(u_ref[pl.ds(HALO, tm), :], wb_ref[...])
    zc = cw_ref[0:1, :] * z_ref[pl.ds(HALO - 2, tm), :]
    zc = zc + cw_ref[1:2, :] * z_ref[pl.ds(HALO - 1, tm), :]
    zc = zc + cw_ref[2:3, :] * z_ref[pl.ds(HALO, tm), :]
    _accumulate_dot(o_ref, mod_ref[2:3, :], (b_gate * zc).astype(BF16), wo_ref)


def _conv_layer(h, mod, layer, gain, w_in, conv_w, w_out, seq):
    t, d = h.shape
    assert conv_w.shape == (CONV_WIDTH, 1, d) and CONV_WIDTH - 1 <= HALO
    tm = _tile(seq, ROW_TILE, HALO)
    tn = _tile(d, FF_TILE, 128)
    nj = d // tn
    halo_blocks = tm // HALO
    w_in = w_in.astype(BF16)
    kern = functools.partial(_conv_kernel, tm=tm, seq=seq)
    return pl.pallas_call(
        kern,
        grid=(t // tm, nj),
        in_specs=[
            pl.BlockSpec((tm, d), lambda i, j: (i, 0)),
            pl.BlockSpec((HALO, d), lambda i, j: (jnp.maximum(i * halo_blocks - 1, 0), 0)),
            pl.BlockSpec((None, None, N_MOD, d), lambda i, j: (layer, (i * tm) // seq, 0, 0)),
            pl.BlockSpec((1, d), lambda i, j: (0, 0)),
            pl.BlockSpec((d, tn), lambda i, j: (0, j)),
            pl.BlockSpec((d, tn), lambda i, j: (0, nj + j)),
            pl.BlockSpec((d, tn), lambda i, j: (0, 2 * nj + j)),
            pl.BlockSpec((CONV_WIDTH, tn), lambda i, j: (0, j)),
            pl.BlockSpec((tn, d), lambda i, j: (j, 0)),
        ],
        out_specs=pl.BlockSpec((tm, d), lambda i, j: (i, 0)),
        out_shape=jax.ShapeDtypeStruct((t, d), F32),
        scratch_shapes=[pltpu.VMEM((tm + HALO, d), BF16), pltpu.VMEM((tm + HALO, tn), F32)],
        compiler_params=_params(("parallel", "arbitrary")),
        name=f"conv_mixer_{layer}",
    )(h, h, mod, gain.reshape(1, d), w_in, w_in, w_in, conv_w.reshape(CONV_WIDTH, d), w_out.astype(BF16))


def _qkv_kernel(h_ref, mod_ref, g_ref, w_ref, o_ref, u_ref, *, q_tiles, q_scale):
    j = pl.program_id(1)

    @pl.when(j == 0)
    def _():
        u_ref[...] = _norm_mod_ref(h_ref, g_ref[...], mod_ref[0:1, :], mod_ref[1:2, :]).astype(BF16)

    scale = jnp.where(j < q_tiles, q_scale, 1.0).astype(F32)
    o_ref[...] = (_dot(u_ref[...], w_ref[...]) * scale).astype(BF16)


def _qkv_proj(h, mod, layer, gain, w_qkv, seq):
    t, d = h.shape
    e = w_qkv.shape[1]
    tm = _tile(seq, PROJ_ROW_TILE, HALO)
    tn = _tile(d, QKV_COL_TILE, 128)
    kern = functools.partial(_qkv_kernel, q_tiles=d // tn, q_scale=HEAD_DIM ** -0.5 * LOG2E)
    return pl.pallas_call(
        kern,
        grid=(t // tm, e // tn),
        in_specs=[
            pl.BlockSpec((tm, d), lambda i, j: (i, 0)),
            pl.BlockSpec((None, None, N_MOD, d), lambda i, j: (layer, (i * tm) // seq, 0, 0)),
            pl.BlockSpec((1, d), lambda i, j: (0, 0)),
            pl.BlockSpec((d, tn), lambda i, j: (0, j)),
        ],
        out_specs=pl.BlockSpec((tm, tn), lambda i, j: (i, j)),
        out_shape=jax.ShapeDtypeStruct((t, e), BF16),
        scratch_shapes=[pltpu.VMEM((tm, d), BF16)],
        compiler_params=_params(("parallel", "arbitrary")),
        name=f"sb_qkv_{layer}",
    )(h, mod, gain.reshape(1, d), w_qkv.astype(BF16))


def _sb_tiles(q_ref, k_ref, v_ref, jobs, carries, tri, causal_mask, ts):
    starts = [pl.multiple_of(jnp.maximum(kb, 0) * ts, ts) for _, kb, _ in jobs]
    zs = [lax.dot_general(q_ref[pl.ds(r * ts, ts), :], k_ref[pl.ds(ks, ts), :],
                          (((1,), (1,)), ((), ())), preferred_element_type=F32)
          for (r, _, _), ks in zip(jobs, starts)]
    log_betas, row_sums, sums = [], [], []
    for (_, _, diagonal), z in zip(jobs, zs):
        neg_abs = pltpu.bitcast(pltpu.bitcast(z, jnp.uint32) | jnp.uint32(1 << 31), F32)
        log_beta = jnp.minimum(z, 0.0) - jnp.log2(1.0 + jnp.exp2(neg_abs))
        log_1m = log_beta - z
        if diagonal:
            log_1m = jnp.where(causal_mask, log_1m, 0.0)
        hi = log_1m.astype(BF16)
        lo = (log_1m - hi.astype(F32)).astype(BF16)
        sums.append(_dot(jnp.concatenate([hi, lo], axis=1), tri))
        log_betas.append(log_beta)
        row_sums.append(jnp.sum(log_1m, axis=1, keepdims=True))
    carries = list(carries)
    pvs = [None] * len(carries)
    for j, (r, kb, diagonal) in enumerate(jobs):
        carry = jnp.where(kb >= 0, carries[r], SB_PARKED_LOG2)
        a = jnp.exp2(log_betas[j] + sums[j] + carry)
        if diagonal:
            a = jnp.where(causal_mask, a, 0.0)
        pv = _dot(a.astype(BF16), v_ref[pl.ds(starts[j], ts), :])
        pvs[r] = pv if pvs[r] is None else pvs[r] + pv
        carries[r] = carry + row_sums[j]
    return carries, pvs


def _sb_attn_kernel(q_ref, k_ref, v_ref, o_ref, acc_ref, *, ts, chains):
    base = pl.program_id(2) * chains
    row = lax.broadcasted_iota(jnp.int32, (ts, ts), 0)
    col = lax.broadcasted_iota(jnp.int32, (ts, ts), 1)
    tri = jnp.tile((row > col).astype(BF16), (2, 1))

    def alive(carries):
        return functools.reduce(jnp.maximum, [jnp.max(cr) for cr in carries]) > SB_DEAD_LOG2

    jobs = [(r, base + r, True) for r in range(chains)] + [(r, base + r - 1, False) for r in range(chains)]
    carries, pvs = _sb_tiles(q_ref, k_ref, v_ref, jobs, [jnp.zeros((ts, 1), F32)] * chains, tri, col < row, ts)
    for r in range(chains):
        acc_ref[pl.ds(r * ts, ts), :] = pvs[r]

    def cond(state):
        it, live = state[0], state[1]
        return jnp.logical_and(it < base + chains, live)

    def body(state):
        it, carries = state[0], state[2:]
        jobs = [(r, base + r - it, False) for r in range(chains)]
        carries, pvs = _sb_tiles(q_ref, k_ref, v_ref, jobs, carries, tri, None, ts)
        for r in range(chains):
            acc_ref[pl.ds(r * ts, ts), :] += pvs[r]
        return (it + 1, alive(carries), *carries)

    lax.while_loop(cond, body, (jnp.int32(2), alive(carries), *carries))
    o_ref[...] = acc_ref[...].astype(BF16)


def _sb_attention(qkv, batch, seq, d):
    t = qkv.shape[0]
    heads = d // HEAD_DIM
    ts = _tile(seq, ATTN_TILE, 128)
    chains = math.gcd(seq // ts, ATTN_CHAINS)
    tq = ts * chains
    nq = seq // tq
    kern = functools.partial(_sb_attn_kernel, ts=ts, chains=chains)
    return pl.pallas_call(
        kern,
        grid=(batch, heads, nq),
        in_specs=[
            pl.BlockSpec((tq, HEAD_DIM), lambda b, hd, qi: (b * nq + qi, hd)),
            pl.BlockSpec((seq, HEAD_DIM), lambda b, hd, qi: (b, heads + hd)),
            pl.BlockSpec((seq, HEAD_DIM), lambda b, hd, qi: (b, 2 * heads + hd)),
        ],
        out_specs=pl.BlockSpec((tq, HEAD_DIM), lambda b, hd, qi: (b * nq + qi, hd)),
        out_shape=jax.ShapeDtypeStruct((t, d), BF16),
        scratch_shapes=[pltpu.VMEM((tq, HEAD_DIM), F32)],
        compiler_params=_params(("parallel", "parallel", "arbitrary")),
        name="sb_attention",
    )(qkv, qkv, qkv)


def _oproj_kernel(a_ref, w_ref, h_ref, mod_ref, o_ref):
    o_ref[...] = h_ref[...] + mod_ref[2:3, :] * _dot(a_ref[...], w_ref[...])


def _out_proj(attn, h, mod, layer, w_o, seq):
    t, d = h.shape
    tm = _tile(seq, PROJ_ROW_TILE, HALO)
    tn = _tile(d, OUT_COL_TILE, 128)
    return pl.pallas_call(
        _oproj_kernel,
        grid=(t // tm, d // tn),
        in_specs=[
            pl.BlockSpec((tm, d), lambda i, j: (i, 0)),
            pl.BlockSpec((d, tn), lambda i, j: (0, j)),
            pl.BlockSpec((tm, tn), lambda i, j: (i, j)),
            pl.BlockSpec((None, None, N_MOD, tn), lambda i, j: (layer, (i * tm) // seq, 0, j)),
        ],
        out_specs=pl.BlockSpec((tm, tn), lambda i, j: (i, j)),
        out_shape=jax.ShapeDtypeStruct((t, d), F32),
        compiler_params=_params(("parallel", "arbitrary")),
        name=f"sb_out_proj_{layer}",
    )(attn, w_o.astype(BF16), h, mod)


def _ffn_kernel(h_hbm, mod_ref, g_ref, wg_ref, wu_ref, wd_ref, fg_ref, o_ref, u_ref, h_ref, h_sem, *,
                final_norm):
    i, f = pl.program_id(0), pl.program_id(1)
    nb, nf = pl.num_programs(0), pl.num_programs(1)
    last = nf - 1
    tm = h_ref.shape[0]

    def h_copy(block):
        return pltpu.make_async_copy(h_hbm.at[pl.ds(pl.multiple_of(block * tm, tm), tm), :], h_ref, h_sem)

    @pl.when(f == 0)
    def _():
        @pl.when(i == 0)
        def _():
            h_copy(0).start()

        h_copy(i).wait()
        u_ref[...] = _norm_mod_ref(h_ref, g_ref[...], mod_ref[3:4, :], mod_ref[4:5, :]).astype(BF16)

    _start_from_residual(o_ref, h_ref, f)

    refill = jnp.minimum(1, last)

    @pl.when(jnp.logical_and(f == refill, i + 1 < nb))
    def _():
        h_copy(i + 1).start()

    u = u_ref[...]
    gate = _dot(u, wg_ref[...])
    up = _dot(u, wu_ref[...])
    act = (gate / (1.0 + jnp.exp(-gate)) * up).astype(BF16)
    _accumulate_dot(o_ref, mod_ref[5:6, :], act, wd_ref)

    if final_norm:
        @pl.when(f == last)
        def _():
            o_ref[...] = _rmsnorm(o_ref[...], fg_ref[...])


def _ffn_layer(h, mod, layer, gain, w_gate, w_up, w_down, final_g, seq, final_norm):
    t, d = h.shape
    ff = w_gate.shape[2]
    tm = _tile(seq, FFN_ROW_TILE, HALO)
    tf = _tile(ff, FF_TILE, 128)
    kern = functools.partial(_ffn_kernel, final_norm=final_norm)
    return pl.pallas_call(
        kern,
        grid=(t // tm, ff // tf),
        in_specs=[
            pl.BlockSpec(memory_space=pl.ANY),
            pl.BlockSpec((None, None, N_MOD, d), lambda i, f: (layer, (i * tm) // seq, 0, 0)),
            pl.BlockSpec((1, d), lambda i, f: (0, 0)),
            pl.BlockSpec((None, d, tf), lambda i, f: (layer, 0, f)),
            pl.BlockSpec((None, d, tf), lambda i, f: (layer, 0, f)),
            pl.BlockSpec((None, tf, d), lambda i, f: (layer, f, 0)),
            pl.BlockSpec((1, d), lambda i, f: (0, 0)),
        ],
        out_specs=pl.BlockSpec((tm, d), lambda i, f: (i, 0)),
        out_shape=jax.ShapeDtypeStruct((t, d), F32),
        scratch_shapes=[pltpu.VMEM((tm, d), BF16), pltpu.VMEM((tm, d), F32), pltpu.SemaphoreType.DMA(())],
        compiler_params=_params(("arbitrary", "arbitrary")),
        name=f"ffn_{layer}",
    )(h, mod, gain.reshape(1, d), w_gate, w_up, w_down, final_g.reshape(1, d))


def kernel(x, c, norm_mix_g, norm_ffn_g, w_mod, b_mod, pool_w, pool_scale, conv_w_in, conv_w,
           conv_w_out, sb_w_qkv, sb_w_o, ffn_w_gate, ffn_w_up, ffn_w_down, final_g):
    batch, seq, d = x.shape
    depth = w_mod.shape[0]
    assert w_mod.shape == (depth, d, N_MOD * d) and d % HEAD_DIM == 0
    mod = _modulation(c, w_mod, b_mod).reshape(depth, batch, N_MOD, d)
    h = x.reshape(batch * seq, d)
    w_gate, w_up, w_down = ffn_w_gate.astype(BF16), ffn_w_up.astype(BF16), ffn_w_down.astype(BF16)
    for i in range(depth):
        kind, j = i % N_MIXERS, i // N_MIXERS
        if kind == 0:
            h = _pool_layer(h, mod, i, norm_mix_g[i], pool_w[j], pool_scale[j], seq)
        elif kind == 1:
            h = _conv_layer(h, mod, i, norm_mix_g[i], conv_w_in[j], conv_w[j], conv_w_out[j], seq)
        else:
            qkv = _qkv_proj(h, mod, i, norm_mix_g[i], sb_w_qkv[j], seq)
            attn = _sb_attention(qkv, batch, seq, d)
            h = _out_proj(attn, h, mod, i, sb_w_o[j], seq)
        h = _ffn_layer(h, mod, i, norm_ffn_g[i], w_gate, w_up, w_down, final_g, seq,
                       final_norm=(i == depth - 1))
    return h.reshape(batch, seq, d)
```

```python
import functools
import math

import jax
import jax.numpy as jnp
from jax import lax
from jax.experimental import pallas as pl
from jax.experimental.pallas import tpu as pltpu

POOL_WINDOWS = (2, 4, 8, 16)
HEAD_DIM = 128
CONV_WIDTH = 3
N_MIXERS = 3
N_MOD = 6
EPS = 1e-6

HALO = 16
SUBLANES = 8
POOL_HALO = 32

ROW_TILE = 512
PROJ_ROW_TILE = 1024
FFN_ROW_TILE = 1024
FF_TILE = 512
QKV_COL_TILE = 1024
OUT_COL_TILE = 1024
MOD_COL_TILE = 1024
ACC_COL_TILE = 512
ATTN_TILE = 256
ATTN_CHAINS = 16
SB_LOGIT_LEAD = 3
SB_SCAN_LEAD = 2

SB_DEAD_LOG2 = 151.0
SB_PARKED_LOG2 = 1e30

V7X_VMEM_BYTES = 64 * 1024 * 1024
VMEM_LIMIT_BYTES = 7 * V7X_VMEM_BYTES // 8

F32 = jnp.float32
BF16 = jnp.bfloat16
LOG2E = math.log2(math.e)


def _tile(n, pref, align):
    if n <= pref:
        return n
    t = (pref // align) * align
    while t >= align:
        if n % t == 0:
            return t
        t -= align
    raise ValueError(f"no {align}-aligned tile <= {pref} divides {n}")


def _params(semantics):
    return pltpu.CompilerParams(dimension_semantics=semantics, vmem_limit_bytes=VMEM_LIMIT_BYTES)


def _rmsnorm(x, g):
    ms = jnp.mean(x * x, axis=-1, keepdims=True)
    return x * lax.rsqrt(ms + EPS) * g


def _norm_mod(x, g, shift, scale):
    return _rmsnorm(x, g) * (1.0 + scale) + shift


def _norm_mod_ref(x_ref, g, shift, scale):
    x = x_ref[...]
    inv = lax.rsqrt(jnp.mean(x * x, axis=-1, keepdims=True) + EPS)
    return x_ref[...] * inv * (g * (1.0 + scale)) + shift


def _dot(a, b):
    return jnp.dot(a, b, preferred_element_type=F32)


def _accumulate_dot(o_ref, base_ref, gate, a, w_ref):
    n = o_ref.shape[1]
    tn = _tile(n, ACC_COL_TILE, 128)
    for c0 in range(0, n, tn):
        cols = slice(c0, c0 + tn)
        o_ref[:, cols] = base_ref[:, cols] + gate[:, cols] * _dot(a, w_ref[:, cols])


def _onto_residual_then_output(step, tile_fn, h_ref, o_ref):
    pl.when(step < 1)(lambda: tile_fn(h_ref))
    pl.when(step >= 1)(lambda: tile_fn(o_ref))


def _mod_kernel(c_ref, w_ref, b_ref, o_ref):
    c = c_ref[...]
    sc = c / (1.0 + jnp.exp(-c))
    o_ref[0] = _dot(sc.astype(BF16), w_ref[0].astype(BF16)) + b_ref[0]


def _modulation(c, w_mod, b_mod):
    depth, d, e = w_mod.shape
    b = c.shape[0]
    tn = _tile(e, MOD_COL_TILE, 128)
    return pl.pallas_call(
        _mod_kernel,
        grid=(depth, e // tn),
        in_specs=[
            pl.BlockSpec((b, d), lambda l, j: (0, 0)),
            pl.BlockSpec((1, d, tn), lambda l, j: (l, 0, j)),
            pl.BlockSpec((1, 1, tn), lambda l, j: (l, 0, j)),
        ],
        out_specs=pl.BlockSpec((1, b, tn), lambda l, j: (l, 0, j)),
        out_shape=jax.ShapeDtypeStruct((depth, b, e), F32),
        compiler_params=_params(("parallel", "parallel")),
        name="adaln_mod",
    )(c, w_mod, b_mod.reshape(depth, 1, e))


def _pool_kernel(h_ref, halo_ref, mod_ref, g_ref, w_ref, ps_ref, o_ref, u_ref, a_ref, b_ref, *, tm, seq, windows):
    i = pl.program_id(0)
    pos0 = (i * tm) % seq
    shift, scale, gate = mod_ref[0:1, :], mod_ref[1:2, :], mod_ref[2:3, :]
    g = g_ref[...]
    u_halo = _norm_mod(halo_ref[...], g, shift, scale)
    u_ref[0:POOL_HALO, :] = jnp.where(pos0 == 0, 0.0, u_halo)
    u_ref[POOL_HALO:, :] = _norm_mod_ref(h_ref, g, shift, scale)
    pos = pos0 + lax.broadcasted_iota(jnp.int32, (tm, 1), 0)
    c = w_ref.shape[1]
    for grp, w in enumerate(windows):
        cols = slice(grp * c, (grp + 1) * c)
        src, span, lo = u_ref, 1, 0
        for dst in (a_ref, b_ref, a_ref, b_ref):
            if span >= w:
                break
            lo += SUBLANES
            n = tm + POOL_HALO - lo
            dst[pl.ds(lo, n), cols] = src[pl.ds(lo, n), cols] + src[pl.ds(lo - span, n), cols]
            src, span = dst, 2 * span
        cur = u_ref[pl.ds(POOL_HALO, tm), cols]
        count = jnp.minimum(pos + 1, w).astype(F32)
        diff = src[pl.ds(POOL_HALO, tm), cols] / count - cur
        y = _dot(diff.astype(BF16), w_ref[grp]) * ps_ref[:, cols]
        o_ref[:, cols] = h_ref[:, cols] + gate[:, cols] * y


def _pool_layer(h, mod, layer, gain, w_pool, pool_scale, seq):
    t, d = h.shape
    groups, c, _ = w_pool.shape
    passes = max(POOL_WINDOWS).bit_length() - 1
    assert groups == len(POOL_WINDOWS) and groups * c == d
    assert all(w & (w - 1) == 0 for w in POOL_WINDOWS) and passes <= 4 and passes * SUBLANES <= POOL_HALO
    tm = _tile(seq, ROW_TILE, POOL_HALO)
    halo_blocks = tm // POOL_HALO
    kern = functools.partial(_pool_kernel, tm=tm, seq=seq, windows=POOL_WINDOWS)
    rows = tm + POOL_HALO
    return pl.pallas_call(
        kern,
        grid=(t // tm,),
        in_specs=[
            pl.BlockSpec((tm, d), lambda i: (i, 0)),
            pl.BlockSpec((POOL_HALO, d), lambda i: (jnp.maximum(i * halo_blocks - 1, 0), 0)),
            pl.BlockSpec((None, None, N_MOD, d), lambda i: (layer, (i * tm) // seq, 0, 0)),
            pl.BlockSpec((1, d), lambda i: (0, 0)),
            pl.BlockSpec((groups, c, c), lambda i: (0, 0, 0)),
            pl.BlockSpec((1, d), lambda i: (0, 0)),
        ],
        out_specs=pl.BlockSpec((tm, d), lambda i: (i, 0)),
        out_shape=jax.ShapeDtypeStruct((t, d), F32),
        scratch_shapes=[pltpu.VMEM((rows, d), F32), pltpu.VMEM((rows, d), F32), pltpu.VMEM((rows, d), F32)],
        compiler_params=_params(("parallel",)),
        name=f"pool_mixer_{layer}",
    )(h, h, mod, gain.reshape(1, d), w_pool.astype(BF16), pool_scale.reshape(1, d))


def _conv_kernel(h_ref, halo_ref, mod_ref, g_ref, wb_ref, wc_ref, wv_ref, cw_ref, wo_ref, o_ref,
                 u_ref, z_ref, *, tm, seq):
    i, j = pl.program_id(0), pl.program_id(1)

    @pl.when(j == 0)
    def _():
        pos0 = (i * tm) % seq
        shift, scale = mod_ref[0:1, :], mod_ref[1:2, :]
        g = g_ref[...]
        u_halo = _norm_mod(halo_ref[...], g, shift, scale)
        u_ref[0:HALO, :] = jnp.where(pos0 == 0, 0.0, u_halo).astype(BF16)
        u_ref[HALO:, :] = _norm_mod_ref(h_ref, g, shift, scale).astype(BF16)

    def channel_tile(base_ref):
        u_all = u_ref[...]
        z_ref[...] = _dot(u_all, wc_ref[...]) * _dot(u_all, wv_ref[...])
        b_gate = _dot(u_ref[pl.ds(HALO, tm), :], wb_ref[...])
        zc = cw_ref[0:1, :] * z_ref[pl.ds(HALO - 2, tm), :]
        zc = zc + cw_ref[1:2, :] * z_ref[pl.ds(HALO - 1, tm), :]
        zc = zc + cw_ref[2:3, :] * z_ref[pl.ds(HALO, tm), :]
        _accumulate_dot(o_ref, base_ref, mod_ref[2:3, :], (b_gate * zc).astype(BF16), wo_ref)

    _onto_residual_then_output(j, channel_tile, h_ref, o_ref)


def _conv_layer(h, mod, layer, gain, w_in, conv_w, w_out, seq):
    t, d = h.shape
    assert conv_w.shape == (CONV_WIDTH, 1, d) and CONV_WIDTH - 1 <= HALO
    tm = _tile(seq, ROW_TILE, HALO)
    tn = _tile(d, FF_TILE, 128)
    nj = d // tn
    halo_blocks = tm // HALO
    w_in = w_in.astype(BF16)
    kern = functools.partial(_conv_kernel, tm=tm, seq=seq)
    return pl.pallas_call(
        kern,
        grid=(t // tm, nj),
        in_specs=[
            pl.BlockSpec((tm, d), lambda i, j: (i, 0)),
            pl.BlockSpec((HALO, d), lambda i, j: (jnp.maximum(i * halo_blocks - 1, 0), 0)),
            pl.BlockSpec((None, None, N_MOD, d), lambda i, j: (layer, (i * tm) // seq, 0, 0)),
            pl.BlockSpec((1, d), lambda i, j: (0, 0)),
            pl.BlockSpec((d, tn), lambda i, j: (0, j)),
            pl.BlockSpec((d, tn), lambda i, j: (0, nj + j)),
            pl.BlockSpec((d, tn), lambda i, j: (0, 2 * nj + j)),
            pl.BlockSpec((CONV_WIDTH, tn), lambda i, j: (0, j)),
            pl.BlockSpec((tn, d), lambda i, j: (j, 0)),
        ],
        out_specs=pl.BlockSpec((tm, d), lambda i, j: (i, 0)),
        out_shape=jax.ShapeDtypeStruct((t, d), F32),
        scratch_shapes=[pltpu.VMEM((tm + HALO, d), BF16), pltpu.VMEM((tm + HALO, tn), F32)],
        compiler_params=_params(("parallel", "arbitrary")),
        name=f"conv_mixer_{layer}",
    )(h, h, mod, gain.reshape(1, d), w_in, w_in, w_in, conv_w.reshape(CONV_WIDTH, d), w_out.astype(BF16))


def _qkv_kernel(h_ref, mod_ref, g_ref, w_ref, o_ref, u_ref, *, q_tiles, q_scale):
    j = pl.program_id(1)

    @pl.when(j == 0)
    def _():
        u_ref[...] = _norm_mod_ref(h_ref, g_ref[...], mod_ref[0:1, :], mod_ref[1:2, :]).astype(BF16)

    scale = jnp.where(j < q_tiles, q_scale, 1.0).astype(F32)
    o_ref[...] = (_dot(u_ref[...], w_ref[...]) * scale).astype(BF16)


def _qkv_proj(h, mod, layer, gain, w_qkv, seq):
    t, d = h.shape
    e = w_qkv.shape[1]
    tm = _tile(seq, PROJ_ROW_TILE, HALO)
    tn = _tile(d, QKV_COL_TILE, 128)
    kern = functools.partial(_qkv_kernel, q_tiles=d // tn, q_scale=HEAD_DIM ** -0.5 * LOG2E)
    return pl.pallas_call(
        kern,
        grid=(t // tm, e // tn),
        in_specs=[
            pl.BlockSpec((tm, d), lambda i, j: (i, 0)),
            pl.BlockSpec((None, None, N_MOD, d), lambda i, j: (layer, (i * tm) // seq, 0, 0)),
            pl.BlockSpec((1, d), lambda i, j: (0, 0)),
            pl.BlockSpec((d, tn), lambda i, j: (0, j)),
        ],
        out_specs=pl.BlockSpec((tm, tn), lambda i, j: (i, j)),
        out_shape=jax.ShapeDtypeStruct((t, e), BF16),
        scratch_shapes=[pltpu.VMEM((tm, d), BF16)],
        compiler_params=_params(("parallel", "arbitrary")),
        name=f"sb_qkv_{layer}",
    )(h, mod, gain.reshape(1, d), w_qkv.astype(BF16))


def _sb_tiles(q_ref, k_ref, v_ref, jobs, carries, tri, causal_mask, ts):
    n = len(jobs)
    starts = [pl.multiple_of(jnp.maximum(kb, 0) * ts, ts) for _, kb, _ in jobs]
    zs, row_sums, sums = [None] * n, [None] * n, [None] * n
    carries = list(carries)
    pvs = [None] * len(carries)

    def logits(j):
        r = jobs[j][0]
        zs[j] = lax.dot_general(q_ref[pl.ds(r * ts, ts), :], k_ref[pl.ds(starts[j], ts), :],
                                (((1,), (1,)), ((), ())), preferred_element_type=F32)

    def scan(j):
        z, diagonal = zs[j], jobs[j][2]
        neg_abs = pltpu.bitcast(pltpu.bitcast(z, jnp.uint32) | jnp.uint32(1 << 31), F32)
        decay = jnp.maximum(z, 0.0) + jnp.log2(1.0 + jnp.exp2(neg_abs))
        if diagonal:
            decay = jnp.where(causal_mask, decay, 0.0)
        hi = decay.astype(BF16)
        lo = (decay - hi.astype(F32)).astype(BF16)
        sums[j] = _dot(jnp.concatenate([hi, lo], axis=1), tri)
        row_sums[j] = jnp.sum(decay, axis=1, keepdims=True)

    def weights(j):
        r, kb, diagonal = jobs[j]
        carry = jnp.where(kb >= 0, carries[r], SB_PARKED_LOG2)
        a = jnp.exp2(zs[j] - sums[j] - carry)
        if diagonal:
            a = jnp.where(causal_mask, a, 0.0)
        pv = _dot(a.astype(BF16), v_ref[pl.ds(starts[j], ts), :])
        pvs[r] = pv if pvs[r] is None else pvs[r] + pv
        carries[r] = carry + row_sums[j]
        zs[j] = None

    for step in range(n + SB_LOGIT_LEAD + SB_SCAN_LEAD):
        if step < n:
            logits(step)
        if 0 <= step - SB_LOGIT_LEAD < n:
            scan(step - SB_LOGIT_LEAD)
        if 0 <= step - SB_LOGIT_LEAD - SB_SCAN_LEAD < n:
            weights(step - SB_LOGIT_LEAD - SB_SCAN_LEAD)
    return carries, pvs


def _sb_attn_kernel(q_ref, k_ref, v_ref, o_ref, acc_ref, *, ts, chains):
    base = pl.program_id(2) * chains
    row = lax.broadcasted_iota(jnp.int32, (ts, ts), 0)
    col = lax.broadcasted_iota(jnp.int32, (ts, ts), 1)
    tri = jnp.tile((row >= col).astype(BF16), (2, 1))

    def alive(carries):
        return functools.reduce(jnp.minimum, [jnp.min(cr) for cr in carries]) < SB_DEAD_LOG2

    jobs = [(r, base + r, True) for r in range(chains)] + [(r, base + r - 1, False) for r in range(chains)]
    carries, pvs = _sb_tiles(q_ref, k_ref, v_ref, jobs, [jnp.zeros((ts, 1), F32)] * chains, tri, col < row, ts)
    for r in range(chains):
        acc_ref[pl.ds(r * ts, ts), :] = pvs[r]

    def cond(state):
        it, live = state[0], state[1]
        return jnp.logical_and(it < base + chains, live)

    def body(state):
        it, carries = state[0], state[2:]
        jobs = [(r, base + r - it, False) for r in range(chains)]
        carries, pvs = _sb_tiles(q_ref, k_ref, v_ref, jobs, carries, tri, None, ts)
        for r in range(chains):
            acc_ref[pl.ds(r * ts, ts), :] += pvs[r]
        return (it + 1, alive(carries), *carries)

    lax.while_loop(cond, body, (jnp.int32(2), alive(carries), *carries))
    o_ref[...] = acc_ref[...].astype(BF16)


def _sb_attention(qkv, batch, seq, d):
    t = qkv.shape[0]
    heads = d // HEAD_DIM
    ts = _tile(seq, ATTN_TILE, 128)
    chains = math.gcd(seq // ts, ATTN_CHAINS)
    tq = ts * chains
    nq = seq // tq
    kern = functools.partial(_sb_attn_kernel, ts=ts, chains=chains)
    return pl.pallas_call(
        kern,
        grid=(batch, heads, nq),
        in_specs=[
            pl.BlockSpec((tq, HEAD_DIM), lambda b, hd, qi: (b * nq + qi, hd)),
            pl.BlockSpec((seq, HEAD_DIM), lambda b, hd, qi: (b, heads + hd)),
            pl.BlockSpec((seq, HEAD_DIM), lambda b, hd, qi: (b, 2 * heads + hd)),
        ],
        out_specs=pl.BlockSpec((tq, HEAD_DIM), lambda b, hd, qi: (b * nq + qi, hd)),
        out_shape=jax.ShapeDtypeStruct((t, d), BF16),
        scratch_shapes=[pltpu.VMEM((tq, HEAD_DIM), F32)],
        compiler_params=_params(("parallel", "parallel", "arbitrary")),
        name="sb_attention",
    )(qkv, qkv, qkv)


def _oproj_kernel(a_ref, w_ref, h_ref, mod_ref, o_ref):
    o_ref[...] = h_ref[...] + mod_ref[2:3, :] * _dot(a_ref[...], w_ref[...])


def _out_proj(attn, h, mod, layer, w_o, seq):
    t, d = h.shape
    tm = _tile(seq, PROJ_ROW_TILE, HALO)
    tn = _tile(d, OUT_COL_TILE, 128)
    return pl.pallas_call(
        _oproj_kernel,
        grid=(t // tm, d // tn),
        in_specs=[
            pl.BlockSpec((tm, d), lambda i, j: (i, 0)),
            pl.BlockSpec((d, tn), lambda i, j: (0, j)),
            pl.BlockSpec((tm, tn), lambda i, j: (i, j)),
            pl.BlockSpec((None, None, N_MOD, tn), lambda i, j: (layer, (i * tm) // seq, 0, j)),
        ],
        out_specs=pl.BlockSpec((tm, tn), lambda i, j: (i, j)),
        out_shape=jax.ShapeDtypeStruct((t, d), F32),
        compiler_params=_params(("parallel", "arbitrary")),
        name=f"sb_out_proj_{layer}",
    )(attn, w_o.astype(BF16), h, mod)


def _ffn_kernel(h_hbm, mod_ref, g_ref, wg_ref, wu_ref, wd_ref, fg_ref, o_ref, u_ref, h_ref, h_sem, *,
                final_norm):
    i, f = pl.program_id(0), pl.program_id(1)
    nb, nf = pl.num_programs(0), pl.num_programs(1)
    last = nf - 1
    tm = h_ref.shape[0]

    def h_copy(block):
        return pltpu.make_async_copy(h_hbm.at[pl.ds(pl.multiple_of(block * tm, tm), tm), :], h_ref, h_sem)

    @pl.when(f == 0)
    def _():
        @pl.when(i == 0)
        def _():
            h_copy(0).start()

        h_copy(i).wait()
        u_ref[...] = _norm_mod_ref(h_ref, g_ref[...], mod_ref[3:4, :], mod_ref[4:5, :]).astype(BF16)

    def hidden_tile(base_ref):
        u = u_ref[...]
        gate = _dot(u, wg_ref[...])
        up = _dot(u, wu_ref[...])
        act = (gate / (1.0 + jnp.exp(-gate)) * up).astype(BF16)
        _accumulate_dot(o_ref, base_ref, mod_ref[5:6, :], act, wd_ref)

    _onto_residual_then_output(f, hidden_tile, h_ref, o_ref)

    refill = jnp.minimum(1, last)

    @pl.when(jnp.logical_and(f == refill, i + 1 < nb))
    def _():
        h_copy(i + 1).start()

    if final_norm:
        @pl.when(f == last)
        def _():
            o_ref[...] = _rmsnorm(o_ref[...], fg_ref[...])


def _ffn_layer(h, mod, layer, gain, w_gate, w_up, w_down, final_g, seq, final_norm):
    t, d = h.shape
    ff = w_gate.shape[2]
    tm = _tile(seq, FFN_ROW_TILE, HALO)
    tf = _tile(ff, FF_TILE, 128)
    kern = functools.partial(_ffn_kernel, final_norm=final_norm)
    return pl.pallas_call(
        kern,
        grid=(t // tm, ff // tf),
        in_specs=[
            pl.BlockSpec(memory_space=pl.ANY),
            pl.BlockSpec((None, None, N_MOD, d), lambda i, f: (layer, (i * tm) // seq, 0, 0)),
            pl.BlockSpec((1, d), lambda i, f: (0, 0)),
            pl.BlockSpec((None, d, tf), lambda i, f: (layer, 0, f)),
            pl.BlockSpec((None, d, tf), lambda i, f: (layer, 0, f)),
            pl.BlockSpec((None, tf, d), lambda i, f: (layer, f, 0)),
            pl.BlockSpec((1, d), lambda i, f: (0, 0)),
        ],
        out_specs=pl.BlockSpec((tm, d), lambda i, f: (i, 0)),
        out_shape=jax.ShapeDtypeStruct((t, d), F32),
        scratch_shapes=[pltpu.VMEM((tm, d), BF16), pltpu.VMEM((tm, d), F32), pltpu.SemaphoreType.DMA(())],
        compiler_params=_params(("arbitrary", "arbitrary")),
        name=f"ffn_{layer}",
    )(h, mod, gain.reshape(1, d), w_gate, w_up, w_down, final_g.reshape(1, d))


def kernel(x, c, norm_mix_g, norm_ffn_g, w_mod, b_mod, pool_w, pool_scale, conv_w_in, conv_w,
           conv_w_out, sb_w_qkv, sb_w_o, ffn_w_gate, ffn_w_up, ffn_w_down, final_g):
    batch, seq, d = x.shape
    depth = w_mod.shape[0]
    assert w_mod.shape == (depth, d, N_MOD * d) and d % HEAD_DIM == 0
    mod = _modulation(c, w_mod, b_mod).reshape(depth, batch, N_MOD, d)
    h = x.reshape(batch * seq, d)
    w_gate, w_up, w_down = ffn_w_gate.astype(BF16), ffn_w_up.astype(BF16), ffn_w_down.astype(BF16)
    for i in range(depth):
        kind, j = i % N_MIXERS, i // N_MIXERS
        if kind == 0:
            h = _pool_layer(h, mod, i, norm_mix_g[i], pool_w[j], pool_scale[j], seq)
        elif kind == 1:
            h = _conv_layer(h, mod, i, norm_mix_g[i], conv_w_in[j], conv_w[j], conv_w_out[j], seq)
        else:
            qkv = _qkv_proj(h, mod, i, norm_mix_g[i], sb_w_qkv[j], seq)
            attn = _sb_attention(qkv, batch, seq, d)
            h = _out_proj(attn, h, mod, i, sb_w_o[j], seq)
        h = _ffn_layer(h, mod, i, norm_ffn_g[i], w_gate, w_up, w_down, final_g, seq,
                       final_norm=(i == depth - 1))
    return h.reshape(batch, seq, d)
```

```python
import functools
import math

import jax
import jax.numpy as jnp
from jax import lax
from jax.experimental import pallas as pl
from jax.experimental.pallas import tpu as pltpu

POOL_WINDOWS = (2, 4, 8, 16)
HEAD_DIM = 128
CONV_WIDTH = 3
N_MIXERS = 3
N_MOD = 6
EPS = 1e-6

HALO = 16
SUBLANES = 8
POOL_HALO = 32

ROW_TILE = 512
PROJ_ROW_TILE = 1024
FFN_ROW_TILE = 1024
FF_TILE = 512
QKV_COL_TILE = 2048
OUT_ROW_TILE = 512
OUT_COL_TILE = 2048
MOD_COL_TILE = 1024
ACC_COL_TILE = 512
ATTN_TILE = 256
ATTN_CHAINS = 16
SB_LOGIT_LEAD = 3
SB_SCAN_LEAD = 2

SB_DEAD_LOG2 = 151.0
SB_PARKED_LOG2 = 1e30

V7X_VMEM_BYTES = 64 * 1024 * 1024
VMEM_LIMIT_BYTES = 7 * V7X_VMEM_BYTES // 8

F32 = jnp.float32
BF16 = jnp.bfloat16
LOG2E = math.log2(math.e)


def _tile(n, pref, align):
    if n <= pref:
        return n
    t = (pref // align) * align
    while t >= align:
        if n % t == 0:
            return t
        t -= align
    raise ValueError(f"no {align}-aligned tile <= {pref} divides {n}")


def _params(semantics):
    return pltpu.CompilerParams(dimension_semantics=semantics, vmem_limit_bytes=VMEM_LIMIT_BYTES)


def _rmsnorm(x, g):
    ms = jnp.mean(x * x, axis=-1, keepdims=True)
    return x * lax.rsqrt(ms + EPS) * g


def _norm_mod(x, g, shift, scale):
    return _rmsnorm(x, g) * (1.0 + scale) + shift


def _norm_mod_ref(x_ref, g, shift, scale):
    x = x_ref[...]
    inv = lax.rsqrt(jnp.mean(x * x, axis=-1, keepdims=True) + EPS)
    return x_ref[...] * inv * (g * (1.0 + scale)) + shift


def _dot(a, b):
    return jnp.dot(a, b, preferred_element_type=F32)


def _accumulate_dot(o_ref, base_ref, gate, a, w_ref):
    n = o_ref.shape[1]
    tn = _tile(n, ACC_COL_TILE, 128)
    for c0 in range(0, n, tn):
        cols = slice(c0, c0 + tn)
        o_ref[:, cols] = base_ref[:, cols] + gate[:, cols] * _dot(a, w_ref[:, cols])


def _onto_residual_then_output(step, tile_fn, h_ref, o_ref):
    pl.when(step < 1)(lambda: tile_fn(h_ref))
    pl.when(step >= 1)(lambda: tile_fn(o_ref))


def _mod_kernel(c_ref, w_ref, b_ref, o_ref):
    c = c_ref[...]
    sc = c / (1.0 + jnp.exp(-c))
    o_ref[0] = _dot(sc.astype(BF16), w_ref[0].astype(BF16)) + b_ref[0]


def _modulation(c, w_mod, b_mod):
    depth, d, e = w_mod.shape
    b = c.shape[0]
    tn = _tile(e, MOD_COL_TILE, 128)
    return pl.pallas_call(
        _mod_kernel,
        grid=(depth, e // tn),
        in_specs=[
            pl.BlockSpec((b, d), lambda l, j: (0, 0)),
            pl.BlockSpec((1, d, tn), lambda l, j: (l, 0, j)),
            pl.BlockSpec((1, 1, tn), lambda l, j: (l, 0, j)),
        ],
        out_specs=pl.BlockSpec((1, b, tn), lambda l, j: (l, 0, j)),
        out_shape=jax.ShapeDtypeStruct((depth, b, e), F32),
        compiler_params=_params(("parallel", "parallel")),
        name="adaln_mod",
    )(c, w_mod, b_mod.reshape(depth, 1, e))


def _pool_kernel(h_ref, halo_ref, mod_ref, g_ref, w_ref, ps_ref, o_ref, u_ref, a_ref, b_ref, *, tm, seq, windows):
    i = pl.program_id(0)
    pos0 = (i * tm) % seq
    shift, scale, gate = mod_ref[0:1, :], mod_ref[1:2, :], mod_ref[2:3, :]
    g = g_ref[...]
    u_halo = _norm_mod(halo_ref[...], g, shift, scale)
    u_ref[0:POOL_HALO, :] = jnp.where(pos0 == 0, 0.0, u_halo)
    u_ref[POOL_HALO:, :] = _norm_mod_ref(h_ref, g, shift, scale)
    pos = pos0 + lax.broadcasted_iota(jnp.int32, (tm, 1), 0)
    c = w_ref.shape[1]
    for grp, w in enumerate(windows):
        cols = slice(grp * c, (grp + 1) * c)
        src, span, lo = u_ref, 1, 0
        for dst in (a_ref, b_ref, a_ref, b_ref):
            if span >= w:
                break
            lo += SUBLANES
            n = tm + POOL_HALO - lo
            dst[pl.ds(lo, n), cols] = src[pl.ds(lo, n), cols] + src[pl.ds(lo - span, n), cols]
            src, span = dst, 2 * span
        cur = u_ref[pl.ds(POOL_HALO, tm), cols]
        count = jnp.minimum(pos + 1, w).astype(F32)
        diff = src[pl.ds(POOL_HALO, tm), cols] / count - cur
        y = _dot(diff.astype(BF16), w_ref[grp]) * ps_ref[:, cols]
        o_ref[:, cols] = h_ref[:, cols] + gate[:, cols] * y


def _pool_layer(h, mod, layer, gain, w_pool, pool_scale, seq):
    t, d = h.shape
    groups, c, _ = w_pool.shape
    passes = max(POOL_WINDOWS).bit_length() - 1
    assert groups == len(POOL_WINDOWS) and groups * c == d
    assert all(w & (w - 1) == 0 for w in POOL_WINDOWS) and passes <= 4 and passes * SUBLANES <= POOL_HALO
    tm = _tile(seq, ROW_TILE, POOL_HALO)
    halo_blocks = tm // POOL_HALO
    kern = functools.partial(_pool_kernel, tm=tm, seq=seq, windows=POOL_WINDOWS)
    rows = tm + POOL_HALO
    return pl.pallas_call(
        kern,
        grid=(t // tm,),
        in_specs=[
            pl.BlockSpec((tm, d), lambda i: (i, 0)),
            pl.BlockSpec((POOL_HALO, d), lambda i: (jnp.maximum(i * halo_blocks - 1, 0), 0)),
            pl.BlockSpec((None, None, N_MOD, d), lambda i: (layer, (i * tm) // seq, 0, 0)),
            pl.BlockSpec((1, d), lambda i: (0, 0)),
            pl.BlockSpec((groups, c, c), lambda i: (0, 0, 0)),
            pl.BlockSpec((1, d), lambda i: (0, 0)),
        ],
        out_specs=pl.BlockSpec((tm, d), lambda i: (i, 0)),
        out_shape=jax.ShapeDtypeStruct((t, d), F32),
        scratch_shapes=[pltpu.VMEM((rows, d), F32), pltpu.VMEM((rows, d), F32), pltpu.VMEM((rows, d), F32)],
        compiler_params=_params(("parallel",)),
        name=f"pool_mixer_{layer}",
    )(h, h, mod, gain.reshape(1, d), w_pool.astype(BF16), pool_scale.reshape(1, d))


def _conv_kernel(h_ref, halo_ref, mod_ref, g_ref, wb_ref, wc_ref, wv_ref, cw_ref, wo_ref, o_ref,
                 u_ref, z_ref, *, tm, seq):
    i, j = pl.program_id(0), pl.program_id(1)

    @pl.when(j == 0)
    def _():
        pos0 = (i * tm) % seq
        shift, scale = mod_ref[0:1, :], mod_ref[1:2, :]
        g = g_ref[...]
        u_halo = _norm_mod(halo_ref[...], g, shift, scale)
        u_ref[0:HALO, :] = jnp.where(pos0 == 0, 0.0, u_halo).astype(BF16)
        u_ref[HALO:, :] = _norm_mod_ref(h_ref, g, shift, scale).astype(BF16)

    def channel_tile(base_ref):
        u_all = u_ref[...]
        z_ref[...] = _dot(u_all, wc_ref[...]) * _dot(u_all, wv_ref[...])
        b_gate = _dot(u_ref[pl.ds(HALO, tm), :], wb_ref[...])
        zc = cw_ref[0:1, :] * z_ref[pl.ds(HALO - 2, tm), :]
        zc = zc + cw_ref[1:2, :] * z_ref[pl.ds(HALO - 1, tm), :]
        zc = zc + cw_ref[2:3, :] * z_ref[pl.ds(HALO, tm), :]
        _accumulate_dot(o_ref, base_ref, mod_ref[2:3, :], (b_gate * zc).astype(BF16), wo_ref)

    _onto_residual_then_output(j, channel_tile, h_ref, o_ref)


def _conv_layer(h, mod, layer, gain, w_in, conv_w, w_out, seq):
    t, d = h.shape
    assert conv_w.shape == (CONV_WIDTH, 1, d) and CONV_WIDTH - 1 <= HALO
    tm = _tile(seq, ROW_TILE, HALO)
    tn = _tile(d, FF_TILE, 128)
    nj = d // tn
    halo_blocks = tm // HALO
    w_in = w_in.astype(BF16)
    kern = functools.partial(_conv_kernel, tm=tm, seq=seq)
    return pl.pallas_call(
        kern,
        grid=(t // tm, nj),
        in_specs=[
            pl.BlockSpec((tm, d), lambda i, j: (i, 0)),
            pl.BlockSpec((HALO, d), lambda i, j: (jnp.maximum(i * halo_blocks - 1, 0), 0)),
            pl.BlockSpec((None, None, N_MOD, d), lambda i, j: (layer, (i * tm) // seq, 0, 0)),
            pl.BlockSpec((1, d), lambda i, j: (0, 0)),
            pl.BlockSpec((d, tn), lambda i, j: (0, j)),
            pl.BlockSpec((d, tn), lambda i, j: (0, nj + j)),
            pl.BlockSpec((d, tn), lambda i, j: (0, 2 * nj + j)),
            pl.BlockSpec((CONV_WIDTH, tn), lambda i, j: (0, j)),
            pl.BlockSpec((tn, d), lambda i, j: (j, 0)),
        ],
        out_specs=pl.BlockSpec((tm, d), lambda i, j: (i, 0)),
        out_shape=jax.ShapeDtypeStruct((t, d), F32),
        scratch_shapes=[pltpu.VMEM((tm + HALO, d), BF16), pltpu.VMEM((tm + HALO, tn), F32)],
        compiler_params=_params(("parallel", "arbitrary")),
        name=f"conv_mixer_{layer}",
    )(h, h, mod, gain.reshape(1, d), w_in, w_in, w_in, conv_w.reshape(CONV_WIDTH, d), w_out.astype(BF16))


def _qkv_kernel(h_ref, mod_ref, g_ref, w_ref, o_ref, u_ref, *, q_tiles, q_scale):
    j = pl.program_id(1)

    @pl.when(j == 0)
    def _():
        u_ref[...] = _norm_mod_ref(h_ref, g_ref[...], mod_ref[0:1, :], mod_ref[1:2, :]).astype(BF16)

    scale = jnp.where(j < q_tiles, q_scale, 1.0).astype(F32)
    u = u_ref[...]
    n = o_ref.shape[1]
    tn = _tile(n, ACC_COL_TILE, 128)
    for c0 in range(0, n, tn):
        o_ref[:, c0:c0 + tn] = (_dot(u, w_ref[:, c0:c0 + tn]) * scale).astype(BF16)


def _qkv_proj(h, mod, layer, gain, w_qkv, seq):
    t, d = h.shape
    e = w_qkv.shape[1]
    tm = _tile(seq, PROJ_ROW_TILE, HALO)
    tn = _tile(d, QKV_COL_TILE, 128)
    kern = functools.partial(_qkv_kernel, q_tiles=d // tn, q_scale=HEAD_DIM ** -0.5 * LOG2E)
    return pl.pallas_call(
        kern,
        grid=(t // tm, e // tn),
        in_specs=[
            pl.BlockSpec((tm, d), lambda i, j: (i, 0)),
            pl.BlockSpec((None, None, N_MOD, d), lambda i, j: (layer, (i * tm) // seq, 0, 0)),
            pl.BlockSpec((1, d), lambda i, j: (0, 0)),
            pl.BlockSpec((d, tn), lambda i, j: (0, j)),
        ],
        out_specs=pl.BlockSpec((tm, tn), lambda i, j: (i, j)),
        out_shape=jax.ShapeDtypeStruct((t, e), BF16),
        scratch_shapes=[pltpu.VMEM((tm, d), BF16)],
        compiler_params=_params(("parallel", "arbitrary")),
        name=f"sb_qkv_{layer}",
    )(h, mod, gain.reshape(1, d), w_qkv.astype(BF16))


def _sb_tiles(q_ref, k_ref, v_ref, jobs, carries, tri, causal_mask, ts):
    n = len(jobs)
    starts = [pl.multiple_of(jnp.maximum(kb, 0) * ts, ts) for _, kb, _ in jobs]
    zs, row_sums, sums = [None] * n, [None] * n, [None] * n
    carries = list(carries)
    pvs = [None] * len(carries)

    def logits(j):
        r = jobs[j][0]
        zs[j] = lax.dot_general(q_ref[pl.ds(r * ts, ts), :], k_ref[pl.ds(starts[j], ts), :],
                                (((1,), (1,)), ((), ())), preferred_element_type=F32)

    def scan(j):
        z, diagonal = zs[j], jobs[j][2]
        neg_abs = pltpu.bitcast(pltpu.bitcast(z, jnp.uint32) | jnp.uint32(1 << 31), F32)
        decay = jnp.maximum(z, 0.0) + jnp.log2(1.0 + jnp.exp2(neg_abs))
        if diagonal:
            decay = jnp.where(causal_mask, decay, 0.0)
        hi = decay.astype(BF16)
        lo = (decay - hi.astype(F32)).astype(BF16)
        sums[j] = _dot(jnp.concatenate([hi, lo], axis=1), tri)
        row_sums[j] = jnp.sum(decay, axis=1, keepdims=True)

    def weights(j):
        r, kb, diagonal = jobs[j]
        carry = jnp.where(kb >= 0, carries[r], SB_PARKED_LOG2)
        a = jnp.exp2(zs[j] - sums[j] - carry)
        if diagonal:
            a = jnp.where(causal_mask, a, 0.0)
        pv = _dot(a.astype(BF16), v_ref[pl.ds(starts[j], ts), :])
        pvs[r] = pv if pvs[r] is None else pvs[r] + pv
        carries[r] = carry + row_sums[j]
        zs[j] = None

    for step in range(n + SB_LOGIT_LEAD + SB_SCAN_LEAD):
        if step < n:
            logits(step)
        if 0 <= step - SB_LOGIT_LEAD < n:
            scan(step - SB_LOGIT_LEAD)
        if 0 <= step - SB_LOGIT_LEAD - SB_SCAN_LEAD < n:
            weights(step - SB_LOGIT_LEAD - SB_SCAN_LEAD)
    return carries, pvs


def _sb_attn_kernel(q_ref, k_ref, v_ref, o_ref, acc_ref, *, ts, chains):
    base = pl.program_id(2) * chains
    row = lax.broadcasted_iota(jnp.int32, (ts, ts), 0)
    col = lax.broadcasted_iota(jnp.int32, (ts, ts), 1)
    tri = jnp.tile((row >= col).astype(BF16), (2, 1))

    def alive(carries):
        return functools.reduce(jnp.minimum, [jnp.min(cr) for cr in carries]) < SB_DEAD_LOG2

    jobs = [(r, base + r, True) for r in range(chains)] + [(r, base + r - 1, False) for r in range(chains)]
    carries, pvs = _sb_tiles(q_ref, k_ref, v_ref, jobs, [jnp.zeros((ts, 1), F32)] * chains, tri, col < row, ts)
    for r in range(chains):
        acc_ref[pl.ds(r * ts, ts), :] = pvs[r]

    def cond(state):
        it, live = state[0], state[1]
        return jnp.logical_and(it < base + chains, live)

    def body(state):
        it, carries = state[0], state[2:]
        jobs = [(r, base + r - it, False) for r in range(chains)]
        carries, pvs = _sb_tiles(q_ref, k_ref, v_ref, jobs, carries, tri, None, ts)
        for r in range(chains):
            acc_ref[pl.ds(r * ts, ts), :] += pvs[r]
        return (it + 1, alive(carries), *carries)

    lax.while_loop(cond, body, (jnp.int32(2), alive(carries), *carries))
    o_ref[...] = acc_ref[...].astype(BF16)


def _sb_attention(qkv, batch, seq, d):
    t = qkv.shape[0]
    heads = d // HEAD_DIM
    ts = _tile(seq, ATTN_TILE, 128)
    chains = math.gcd(seq // ts, ATTN_CHAINS)
    tq = ts * chains
    nq = seq // tq
    kern = functools.partial(_sb_attn_kernel, ts=ts, chains=chains)
    return pl.pallas_call(
        kern,
        grid=(batch, heads, nq),
        in_specs=[
            pl.BlockSpec((tq, HEAD_DIM), lambda b, hd, qi: (b * nq + qi, hd)),
            pl.BlockSpec((seq, HEAD_DIM), lambda b, hd, qi: (b, heads + hd)),
            pl.BlockSpec((seq, HEAD_DIM), lambda b, hd, qi: (b, 2 * heads + hd)),
        ],
        out_specs=pl.BlockSpec((tq, HEAD_DIM), lambda b, hd, qi: (b * nq + qi, hd)),
        out_shape=jax.ShapeDtypeStruct((t, d), BF16),
        scratch_shapes=[pltpu.VMEM((tq, HEAD_DIM), F32)],
        compiler_params=_params(("parallel", "parallel", "arbitrary")),
        name="sb_attention",
    )(qkv, qkv, qkv)


def _oproj_kernel(a_ref, w_ref, h_ref, mod_ref, o_ref):
    _accumulate_dot(o_ref, h_ref, mod_ref[2:3, :], a_ref[...], w_ref)


def _out_proj(attn, h, mod, layer, w_o, seq):
    t, d = h.shape
    tm = _tile(seq, OUT_ROW_TILE, HALO)
    tn = _tile(d, OUT_COL_TILE, 128)
    return pl.pallas_call(
        _oproj_kernel,
        grid=(t // tm, d // tn),
        in_specs=[
            pl.BlockSpec((tm, d), lambda i, j: (i, 0)),
            pl.BlockSpec((d, tn), lambda i, j: (0, j)),
            pl.BlockSpec((tm, tn), lambda i, j: (i, j)),
            pl.BlockSpec((None, None, N_MOD, tn), lambda i, j: (layer, (i * tm) // seq, 0, j)),
        ],
        out_specs=pl.BlockSpec((tm, tn), lambda i, j: (i, j)),
        out_shape=jax.ShapeDtypeStruct((t, d), F32),
        compiler_params=_params(("parallel", "arbitrary")),
        name=f"sb_out_proj_{layer}",
    )(attn, w_o.astype(BF16), h, mod)


def _ffn_kernel(h_hbm, mod_ref, g_ref, wg_ref, wu_ref, wd_ref, fg_ref, o_ref, u_ref, h_ref, h_sem, *,
                final_norm):
    i, f = pl.program_id(0), pl.program_id(1)
    nb, nf = pl.num_programs(0), pl.num_programs(1)
    last = nf - 1
    tm = h_ref.shape[0]

    def h_copy(block):
        return pltpu.make_async_copy(h_hbm.at[pl.ds(pl.multiple_of(block * tm, tm), tm), :], h_ref, h_sem)

    @pl.when(f == 0)
    def _():
        @pl.when(i == 0)
        def _():
            h_copy(0).start()

        h_copy(i).wait()
        u_ref[...] = _norm_mod_ref(h_ref, g_ref[...], mod_ref[3:4, :], mod_ref[4:5, :]).astype(BF16)

    def hidden_tile(base_ref):
        u = u_ref[...]
        gate = _dot(u, wg_ref[...])
        up = _dot(u, wu_ref[...])
        act = (gate / (1.0 + jnp.exp(-gate)) * up).astype(BF16)
        _accumulate_dot(o_ref, base_ref, mod_ref[5:6, :], act, wd_ref)

    _onto_residual_then_output(f, hidden_tile, h_ref, o_ref)

    refill = jnp.minimum(1, last)

    @pl.when(jnp.logical_and(f == refill, i + 1 < nb))
    def _():
        h_copy(i + 1).start()

    if final_norm:
        @pl.when(f == last)
        def _():
            o_ref[...] = _rmsnorm(o_ref[...], fg_ref[...])


def _ffn_layer(h, mod, layer, gain, w_gate, w_up, w_down, final_g, seq, final_norm):
    t, d = h.shape
    ff = w_gate.shape[2]
    tm = _tile(seq, FFN_ROW_TILE, HALO)
    tf = _tile(ff, FF_TILE, 128)
    kern = functools.partial(_ffn_kernel, final_norm=final_norm)
    return pl.pallas_call(
        kern,
        grid=(t // tm, ff // tf),
        in_specs=[
            pl.BlockSpec(memory_space=pl.ANY),
            pl.BlockSpec((None, None, N_MOD, d), lambda i, f: (layer, (i * tm) // seq, 0, 0)),
            pl.BlockSpec((1, d), lambda i, f: (0, 0)),
            pl.BlockSpec((None, d, tf), lambda i, f: (layer, 0, f)),
            pl.BlockSpec((None, d, tf), lambda i, f: (layer, 0, f)),
            pl.BlockSpec((None, tf, d), lambda i, f: (layer, f, 0)),
            pl.BlockSpec((1, d), lambda i, f: (0, 0)),
        ],
        out_specs=pl.BlockSpec((tm, d), lambda i, f: (i, 0)),
        out_shape=jax.ShapeDtypeStruct((t, d), F32),
        scratch_shapes=[pltpu.VMEM((tm, d), BF16), pltpu.VMEM((tm, d), F32), pltpu.SemaphoreType.DMA(())],
        compiler_params=_params(("arbitrary", "arbitrary")),
        name=f"ffn_{layer}",
    )(h, mod, gain.reshape(1, d), w_gate, w_up, w_down, final_g.reshape(1, d))


def kernel(x, c, norm_mix_g, norm_ffn_g, w_mod, b_mod, pool_w, pool_scale, conv_w_in, conv_w,
           conv_w_out, sb_w_qkv, sb_w_o, ffn_w_gate, ffn_w_up, ffn_w_down, final_g):
    batch, seq, d = x.shape
    depth = w_mod.shape[0]
    assert w_mod.shape == (depth, d, N_MOD * d) and d % HEAD_DIM == 0
    mod = _modulation(c, w_mod, b_mod).reshape(depth, batch, N_MOD, d)
    h = x.reshape(batch * seq, d)
    w_gate, w_up, w_down = ffn_w_gate.astype(BF16), ffn_w_up.astype(BF16), ffn_w_down.astype(BF16)
    for i in range(depth):
        kind, j = i % N_MIXERS, i // N_MIXERS
        if kind == 0:
            h = _pool_layer(h, mod, i, norm_mix_g[i], pool_w[j], pool_scale[j], seq)
        elif kind == 1:
            h = _conv_layer(h, mod, i, norm_mix_g[i], conv_w_in[j], conv_w[j], conv_w_out[j], seq)
        else:
            qkv = _qkv_proj(h, mod, i, norm_mix_g[i], sb_w_qkv[j], seq)
            attn = _sb_attention(qkv, batch, seq, d)
            h = _out_proj(attn, h, mod, i, sb_w_o[j], seq)
        h = _ffn_layer(h, mod, i, norm_ffn_g[i], w_gate, w_up, w_down, final_g, seq,
                       final_norm=(i == depth - 1))
    return h.reshape(batch, seq, d)
```

```python
import functools
import math

import jax
import jax.numpy as jnp
from jax import lax
from jax.experimental import pallas as pl
from jax.experimental.pallas import tpu as pltpu

POOL_WINDOWS = (2, 4, 8, 16)
HEAD_DIM = 128
CONV_WIDTH = 3
N_MIXERS = 3
N_MOD = 6
EPS = 1e-6

HALO = 16
SUBLANES = 8
POOL_HALO = 32

ROW_TILE = 512
PROJ_ROW_TILE = 1024
FFN_ROW_TILE = 1024
FF_TILE = 512
CONV_COL_TILE = 1024
CONV_SUB_TILE = 512
QKV_COL_TILE = 2048
OUT_ROW_TILE = 512
OUT_COL_TILE = 2048
MOD_COL_TILE = 1024
ACC_COL_TILE = 512
ATTN_TILE = 256
ATTN_CHAINS = 16
SB_LOGIT_LEAD = 3
SB_SCAN_LEAD = 2

SB_DEAD_LOG2 = 151.0
SB_PARKED_LOG2 = 1e30

V7X_VMEM_BYTES = 64 * 1024 * 1024
VMEM_LIMIT_BYTES = 7 * V7X_VMEM_BYTES // 8

F32 = jnp.float32
BF16 = jnp.bfloat16
LOG2E = math.log2(math.e)


def _tile(n, pref, align):
    if n <= pref:
        return n
    t = (pref // align) * align
    while t >= align:
        if n % t == 0:
            return t
        t -= align
    raise ValueError(f"no {align}-aligned tile <= {pref} divides {n}")


def _params(semantics):
    return pltpu.CompilerParams(dimension_semantics=semantics, vmem_limit_bytes=VMEM_LIMIT_BYTES)


def _rmsnorm(x, g):
    ms = jnp.mean(x * x, axis=-1, keepdims=True)
    return x * lax.rsqrt(ms + EPS) * g


def _norm_mod(x, g, shift, scale):
    return _rmsnorm(x, g) * (1.0 + scale) + shift


def _norm_mod_ref(x_ref, g, shift, scale):
    x = x_ref[...]
    inv = lax.rsqrt(jnp.mean(x * x, axis=-1, keepdims=True) + EPS)
    return x_ref[...] * inv * (g * (1.0 + scale)) + shift


def _dot(a, b):
    return jnp.dot(a, b, preferred_element_type=F32)


def _accumulate_dot(o_ref, base_ref, gate, a, w_ref):
    n = o_ref.shape[1]
    tn = _tile(n, ACC_COL_TILE, 128)
    for c0 in range(0, n, tn):
        cols = slice(c0, c0 + tn)
        o_ref[:, cols] = base_ref[:, cols] + gate[:, cols] * _dot(a, w_ref[:, cols])


def _onto_residual_then_output(step, tile_fn, h_ref, o_ref):
    pl.when(step < 1)(lambda: tile_fn(h_ref))
    pl.when(step >= 1)(lambda: tile_fn(o_ref))


def _mod_kernel(c_ref, w_ref, b_ref, o_ref):
    c = c_ref[...]
    sc = c / (1.0 + jnp.exp(-c))
    o_ref[0] = _dot(sc.astype(BF16), w_ref[0].astype(BF16)) + b_ref[0]


def _modulation(c, w_mod, b_mod):
    depth, d, e = w_mod.shape
    b = c.shape[0]
    tn = _tile(e, MOD_COL_TILE, 128)
    return pl.pallas_call(
        _mod_kernel,
        grid=(depth, e // tn),
        in_specs=[
            pl.BlockSpec((b, d), lambda l, j: (0, 0)),
            pl.BlockSpec((1, d, tn), lambda l, j: (l, 0, j)),
            pl.BlockSpec((1, 1, tn), lambda l, j: (l, 0, j)),
        ],
        out_specs=pl.BlockSpec((1, b, tn), lambda l, j: (l, 0, j)),
        out_shape=jax.ShapeDtypeStruct((depth, b, e), F32),
        compiler_params=_params(("parallel", "parallel")),
        name="adaln_mod",
    )(c, w_mod, b_mod.reshape(depth, 1, e))


def _pool_kernel(h_ref, halo_ref, mod_ref, g_ref, w_ref, ps_ref, o_ref, u_ref, a_ref, b_ref, *, tm, seq, windows):
    i = pl.program_id(0)
    pos0 = (i * tm) % seq
    shift, scale, gate = mod_ref[0:1, :], mod_ref[1:2, :], mod_ref[2:3, :]
    g = g_ref[...]
    u_halo = _norm_mod(halo_ref[...], g, shift, scale)
    u_ref[0:POOL_HALO, :] = jnp.where(pos0 == 0, 0.0, u_halo)
    u_ref[POOL_HALO:, :] = _norm_mod_ref(h_ref, g, shift, scale)
    pos = pos0 + lax.broadcasted_iota(jnp.int32, (tm, 1), 0)
    c = w_ref.shape[1]
    for grp, w in enumerate(windows):
        cols = slice(grp * c, (grp + 1) * c)
        src, span, lo = u_ref, 1, 0
        for dst in (a_ref, b_ref, a_ref, b_ref):
            if span >= w:
                break
            lo += SUBLANES
            n = tm + POOL_HALO - lo
            dst[pl.ds(lo, n), cols] = src[pl.ds(lo, n), cols] + src[pl.ds(lo - span, n), cols]
            src, span = dst, 2 * span
        cur = u_ref[pl.ds(POOL_HALO, tm), cols]
        count = jnp.minimum(pos + 1, w).astype(F32)
        diff = src[pl.ds(POOL_HALO, tm), cols] / count - cur
        y = _dot(diff.astype(BF16), w_ref[grp]) * ps_ref[:, cols]
        o_ref[:, cols] = h_ref[:, cols] + gate[:, cols] * y


def _pool_layer(h, mod, layer, gain, w_pool, pool_scale, seq):
    t, d = h.shape
    groups, c, _ = w_pool.shape
    passes = max(POOL_WINDOWS).bit_length() - 1
    assert groups == len(POOL_WINDOWS) and groups * c == d
    assert all(w & (w - 1) == 0 for w in POOL_WINDOWS) and passes <= 4 and passes * SUBLANES <= POOL_HALO
    tm = _tile(seq, ROW_TILE, POOL_HALO)
    halo_blocks = tm // POOL_HALO
    kern = functools.partial(_pool_kernel, tm=tm, seq=seq, windows=POOL_WINDOWS)
    rows = tm + POOL_HALO
    return pl.pallas_call(
        kern,
        grid=(t // tm,),
        in_specs=[
            pl.BlockSpec((tm, d), lambda i: (i, 0)),
            pl.BlockSpec((POOL_HALO, d), lambda i: (jnp.maximum(i * halo_blocks - 1, 0), 0)),
            pl.BlockSpec((None, None, N_MOD, d), lambda i: (layer, (i * tm) // seq, 0, 0)),
            pl.BlockSpec((1, d), lambda i: (0, 0)),
            pl.BlockSpec((groups, c, c), lambda i: (0, 0, 0)),
            pl.BlockSpec((1, d), lambda i: (0, 0)),
        ],
        out_specs=pl.BlockSpec((tm, d), lambda i: (i, 0)),
        out_shape=jax.ShapeDtypeStruct((t, d), F32),
        scratch_shapes=[pltpu.VMEM((rows, d), F32), pltpu.VMEM((rows, d), F32), pltpu.VMEM((rows, d), F32)],
        compiler_params=_params(("parallel",)),
        name=f"pool_mixer_{layer}",
    )(h, h, mod, gain.reshape(1, d), w_pool.astype(BF16), pool_scale.reshape(1, d))


def _conv_kernel(h_ref, halo_ref, mod_ref, g_ref, wb_ref, wc_ref, wv_ref, cw_ref, wo_ref, o_ref,
                 u_ref, z_ref, *, tm, seq):
    i, j = pl.program_id(0), pl.program_id(1)

    @pl.when(j == 0)
    def _():
        pos0 = (i * tm) % seq
        shift, scale = mod_ref[0:1, :], mod_ref[1:2, :]
        g = g_ref[...]
        u_halo = _norm_mod(halo_ref[...], g, shift, scale)
        u_ref[0:HALO, :] = jnp.where(pos0 == 0, 0.0, u_halo).astype(BF16)
        u_ref[HALO:, :] = _norm_mod_ref(h_ref, g, shift, scale).astype(BF16)

    def channel_tile(base_ref):
        u_all = u_ref[...]
        tn = z_ref.shape[1]
        sub = _tile(tn, CONV_SUB_TILE, 128)
        gated = []
        for c0 in range(0, tn, sub):
            cols = slice(c0, c0 + sub)
            z_ref[:, cols] = _dot(u_all, wc_ref[:, cols]) * _dot(u_all, wv_ref[:, cols])
            b_gate = _dot(u_ref[pl.ds(HALO, tm), :], wb_ref[:, cols])
            zc = cw_ref[0:1, cols] * z_ref[pl.ds(HALO - 2, tm), cols]
            zc = zc + cw_ref[1:2, cols] * z_ref[pl.ds(HALO - 1, tm), cols]
            zc = zc + cw_ref[2:3, cols] * z_ref[pl.ds(HALO, tm), cols]
            gated.append((b_gate * zc).astype(BF16))
        _accumulate_dot(o_ref, base_ref, mod_ref[2:3, :], jnp.concatenate(gated, axis=1), wo_ref)

    _onto_residual_then_output(j, channel_tile, h_ref, o_ref)


def _conv_layer(h, mod, layer, gain, w_in, conv_w, w_out, seq):
    t, d = h.shape
    assert conv_w.shape == (CONV_WIDTH, 1, d) and CONV_WIDTH - 1 <= HALO
    tm = _tile(seq, ROW_TILE, HALO)
    tn = _tile(d, CONV_COL_TILE, 128)
    nj = d // tn
    halo_blocks = tm // HALO
    w_in = w_in.astype(BF16)
    kern = functools.partial(_conv_kernel, tm=tm, seq=seq)
    return pl.pallas_call(
        kern,
        grid=(t // tm, nj),
        in_specs=[
            pl.BlockSpec((tm, d), lambda i, j: (i, 0)),
            pl.BlockSpec((HALO, d), lambda i, j: (jnp.maximum(i * halo_blocks - 1, 0), 0)),
            pl.BlockSpec((None, None, N_MOD, d), lambda i, j: (layer, (i * tm) // seq, 0, 0)),
            pl.BlockSpec((1, d), lambda i, j: (0, 0)),
            pl.BlockSpec((d, tn), lambda i, j: (0, j)),
            pl.BlockSpec((d, tn), lambda i, j: (0, nj + j)),
            pl.BlockSpec((d, tn), lambda i, j: (0, 2 * nj + j)),
            pl.BlockSpec((CONV_WIDTH, tn), lambda i, j: (0, j)),
            pl.BlockSpec((tn, d), lambda i, j: (j, 0)),
        ],
        out_specs=pl.BlockSpec((tm, d), lambda i, j: (i, 0)),
        out_shape=jax.ShapeDtypeStruct((t, d), F32),
        scratch_shapes=[pltpu.VMEM((tm + HALO, d), BF16), pltpu.VMEM((tm + HALO, tn), F32)],
        compiler_params=_params(("parallel", "arbitrary")),
        name=f"conv_mixer_{layer}",
    )(h, h, mod, gain.reshape(1, d), w_in, w_in, w_in, conv_w.reshape(CONV_WIDTH, d), w_out.astype(BF16))


def _qkv_kernel(h_ref, mod_ref, g_ref, w_ref, o_ref, u_ref, *, q_tiles, q_scale):
    j = pl.program_id(1)

    @pl.when(j == 0)
    def _():
        u_ref[...] = _norm_mod_ref(h_ref, g_ref[...], mod_ref[0:1, :], mod_ref[1:2, :]).astype(BF16)

    scale = jnp.where(j < q_tiles, q_scale, 1.0).astype(F32)
    u = u_ref[...]
    n = o_ref.shape[1]
    tn = _tile(n, ACC_COL_TILE, 128)
    for c0 in range(0, n, tn):
        o_ref[:, c0:c0 + tn] = (_dot(u, w_ref[:, c0:c0 + tn]) * scale).astype(BF16)


def _qkv_proj(h, mod, layer, gain, w_qkv, seq):
    t, d = h.shape
    e = w_qkv.shape[1]
    tm = _tile(seq, PROJ_ROW_TILE, HALO)
    tn = _tile(d, QKV_COL_TILE, 128)
    kern = functools.partial(_qkv_kernel, q_tiles=d // tn, q_scale=HEAD_DIM ** -0.5 * LOG2E)
    return pl.pallas_call(
        kern,
        grid=(t // tm, e // tn),
        in_specs=[
            pl.BlockSpec((tm, d), lambda i, j: (i, 0)),
            pl.BlockSpec((None, None, N_MOD, d), lambda i, j: (layer, (i * tm) // seq, 0, 0)),
            pl.BlockSpec((1, d), lambda i, j: (0, 0)),
            pl.BlockSpec((d, tn), lambda i, j: (0, j)),
        ],
        out_specs=pl.BlockSpec((tm, tn), lambda i, j: (i, j)),
        out_shape=jax.ShapeDtypeStruct((t, e), BF16),
        scratch_shapes=[pltpu.VMEM((tm, d), BF16)],
        compiler_params=_params(("parallel", "arbitrary")),
        name=f"sb_qkv_{layer}",
    )(h, mod, gain.reshape(1, d), w_qkv.astype(BF16))


def _sb_tiles(q_ref, k_ref, v_ref, jobs, carries, tri, causal_mask, ts):
    n = len(jobs)
    starts = [pl.multiple_of(jnp.maximum(kb, 0) * ts, ts) for _, kb, _ in jobs]
    zs, row_sums, sums = [None] * n, [None] * n, [None] * n
    carries = list(carries)
    pvs = [None] * len(carries)

    def logits(j):
        r = jobs[j][0]
        zs[j] = lax.dot_general(q_ref[pl.ds(r * ts, ts), :], k_ref[pl.ds(starts[j], ts), :],
                                (((1,), (1,)), ((), ())), preferred_element_type=F32)

    def scan(j):
        z, diagonal = zs[j], jobs[j][2]
        neg_abs = pltpu.bitcast(pltpu.bitcast(z, jnp.uint32) | jnp.uint32(1 << 31), F32)
        decay = jnp.maximum(z, 0.0) + jnp.log2(1.0 + jnp.exp2(neg_abs))
        if diagonal:
            decay = jnp.where(causal_mask, decay, 0.0)
        hi = decay.astype(BF16)
        lo = (decay - hi.astype(F32)).astype(BF16)
        sums[j] = _dot(jnp.concatenate([hi, lo], axis=1), tri)
        row_sums[j] = jnp.sum(decay, axis=1, keepdims=True)

    def weights(j):
        r, kb, diagonal = jobs[j]
        carry = jnp.where(kb >= 0, carries[r], SB_PARKED_LOG2)
        a = jnp.exp2(zs[j] - sums[j] - carry)
        if diagonal:
            a = jnp.where(causal_mask, a, 0.0)
        pv = _dot(a.astype(BF16), v_ref[pl.ds(starts[j], ts), :])
        pvs[r] = pv if pvs[r] is None else pvs[r] + pv
        carries[r] = carry + row_sums[j]
        zs[j] = None

    for step in range(n + SB_LOGIT_LEAD + SB_SCAN_LEAD):
        if step < n:
            logits(step)
        if 0 <= step - SB_LOGIT_LEAD < n:
            scan(step - SB_LOGIT_LEAD)
        if 0 <= step - SB_LOGIT_LEAD - SB_SCAN_LEAD < n:
            weights(step - SB_LOGIT_LEAD - SB_SCAN_LEAD)
    return carries, pvs


def _sb_attn_kernel(q_ref, k_ref, v_ref, o_ref, acc_ref, *, ts, chains):
    base = pl.program_id(2) * chains
    row = lax.broadcasted_iota(jnp.int32, (ts, ts), 0)
    col = lax.broadcasted_iota(jnp.int32, (ts, ts), 1)
    tri = jnp.tile((row >= col).astype(BF16), (2, 1))

    def alive(carries):
        return functools.reduce(jnp.minimum, [jnp.min(cr) for cr in carries]) < SB_DEAD_LOG2

    jobs = [(r, base + r, True) for r in range(chains)] + [(r, base + r - 1, False) for r in range(chains)]
    carries, pvs = _sb_tiles(q_ref, k_ref, v_ref, jobs, [jnp.zeros((ts, 1), F32)] * chains, tri, col < row, ts)
    for r in range(chains):
        acc_ref[pl.ds(r * ts, ts), :] = pvs[r]

    def cond(state):
        it, live = state[0], state[1]
        return jnp.logical_and(it < base + chains, live)

    def body(state):
        it, carries = state[0], state[2:]
        jobs = [(r, base + r - it, False) for r in range(chains)]
        carries, pvs = _sb_tiles(q_ref, k_ref, v_ref, jobs, carries, tri, None, ts)
        for r in range(chains):
            acc_ref[pl.ds(r * ts, ts), :] += pvs[r]
        return (it + 1, alive(carries), *carries)

    lax.while_loop(cond, body, (jnp.int32(2), alive(carries), *carries))
    o_ref[...] = acc_ref[...].astype(BF16)


def _sb_attention(qkv, batch, seq, d):
    t = qkv.shape[0]
    heads = d // HEAD_DIM
    ts = _tile(seq, ATTN_TILE, 128)
    chains = math.gcd(seq // ts, ATTN_CHAINS)
    tq = ts * chains
    nq = seq // tq
    kern = functools.partial(_sb_attn_kernel, ts=ts, chains=chains)
    return pl.pallas_call(
        kern,
        grid=(batch, heads, nq),
        in_specs=[
            pl.BlockSpec((tq, HEAD_DIM), lambda b, hd, qi: (b * nq + qi, hd)),
            pl.BlockSpec((seq, HEAD_DIM), lambda b, hd, qi: (b, heads + hd)),
            pl.BlockSpec((seq, HEAD_DIM), lambda b, hd, qi: (b, 2 * heads + hd)),
        ],
        out_specs=pl.BlockSpec((tq, HEAD_DIM), lambda b, hd, qi: (b * nq + qi, hd)),
        out_shape=jax.ShapeDtypeStruct((t, d), BF16),
        scratch_shapes=[pltpu.VMEM((tq, HEAD_DIM), F32)],
        compiler_params=_params(("parallel", "parallel", "arbitrary")),
        name="sb_attention",
    )(qkv, qkv, qkv)


def _oproj_kernel(a_ref, w_ref, h_ref, mod_ref, o_ref):
    _accumulate_dot(o_ref, h_ref, mod_ref[2:3, :], a_ref[...], w_ref)


def _out_proj(attn, h, mod, layer, w_o, seq):
    t, d = h.shape
    tm = _tile(seq, OUT_ROW_TILE, HALO)
    tn = _tile(d, OUT_COL_TILE, 128)
    return pl.pallas_call(
        _oproj_kernel,
        grid=(t // tm, d // tn),
        in_specs=[
            pl.BlockSpec((tm, d), lambda i, j: (i, 0)),
            pl.BlockSpec((d, tn), lambda i, j: (0, j)),
            pl.BlockSpec((tm, tn), lambda i, j: (i, j)),
            pl.BlockSpec((None, None, N_MOD, tn), lambda i, j: (layer, (i * tm) // seq, 0, j)),
        ],
        out_specs=pl.BlockSpec((tm, tn), lambda i, j: (i, j)),
        out_shape=jax.ShapeDtypeStruct((t, d), F32),
        compiler_params=_params(("parallel", "arbitrary")),
        name=f"sb_out_proj_{layer}",
    )(attn, w_o.astype(BF16), h, mod)


def _ffn_kernel(h_hbm, mod_ref, g_ref, wg_ref, wu_ref, wd_ref, fg_ref, o_ref, u_ref, h_ref, h_sem, *,
                final_norm):
    i, f = pl.program_id(0), pl.program_id(1)
    nb, nf = pl.num_programs(0), pl.num_programs(1)
    last = nf - 1
    tm = h_ref.shape[0]

    def h_copy(block):
        return pltpu.make_async_copy(h_hbm.at[pl.ds(pl.multiple_of(block * tm, tm), tm), :], h_ref, h_sem)

    @pl.when(f == 0)
    def _():
        @pl.when(i == 0)
        def _():
            h_copy(0).start()

        h_copy(i).wait()
        u_ref[...] = _norm_mod_ref(h_ref, g_ref[...], mod_ref[3:4, :], mod_ref[4:5, :]).astype(BF16)

    def hidden_tile(base_ref):
        u = u_ref[...]
        gate = _dot(u, wg_ref[...])
        up = _dot(u, wu_ref[...])
        act = (gate / (1.0 + jnp.exp(-gate)) * up).astype(BF16)
        _accumulate_dot(o_ref, base_ref, mod_ref[5:6, :], act, wd_ref)

    _onto_residual_then_output(f, hidden_tile, h_ref, o_ref)

    refill = jnp.minimum(1, last)

    @pl.when(jnp.logical_and(f == refill, i + 1 < nb))
    def _():
        h_copy(i + 1).start()

    if final_norm:
        @pl.when(f == last)
        def _():
            o_ref[...] = _rmsnorm(o_ref[...], fg_ref[...])


def _ffn_layer(h, mod, layer, gain, w_gate, w_up, w_down, final_g, seq, final_norm):
    t, d = h.shape
    ff = w_gate.shape[2]
    tm = _tile(seq, FFN_ROW_TILE, HALO)
    tf = _tile(ff, FF_TILE, 128)
    kern = functools.partial(_ffn_kernel, final_norm=final_norm)
    return pl.pallas_call(
        kern,
        grid=(t // tm, ff // tf),
        in_specs=[
            pl.BlockSpec(memory_space=pl.ANY),
            pl.BlockSpec((None, None, N_MOD, d), lambda i, f: (layer, (i * tm) // seq, 0, 0)),
            pl.BlockSpec((1, d), lambda i, f: (0, 0)),
            pl.BlockSpec((None, d, tf), lambda i, f: (layer, 0, f)),
            pl.BlockSpec((None, d, tf), lambda i, f: (layer, 0, f)),
            pl.BlockSpec((None, tf, d), lambda i, f: (layer, f, 0)),
            pl.BlockSpec((1, d), lambda i, f: (0, 0)),
        ],
        out_specs=pl.BlockSpec((tm, d), lambda i, f: (i, 0)),
        out_shape=jax.ShapeDtypeStruct((t, d), F32),
        scratch_shapes=[pltpu.VMEM((tm, d), BF16), pltpu.VMEM((tm, d), F32), pltpu.SemaphoreType.DMA(())],
        compiler_params=_params(("arbitrary", "arbitrary")),
        name=f"ffn_{layer}",
    )(h, mod, gain.reshape(1, d), w_gate, w_up, w_down, final_g.reshape(1, d))


def kernel(x, c, norm_mix_g, norm_ffn_g, w_mod, b_mod, pool_w, pool_scale, conv_w_in, conv_w,
           conv_w_out, sb_w_qkv, sb_w_o, ffn_w_gate, ffn_w_up, ffn_w_down, final_g):
    batch, seq, d = x.shape
    depth = w_mod.shape[0]
    assert w_mod.shape == (depth, d, N_MOD * d) and d % HEAD_DIM == 0
    mod = _modulation(c, w_mod, b_mod).reshape(depth, batch, N_MOD, d)
    h = x.reshape(batch * seq, d)
    w_gate, w_up, w_down = ffn_w_gate.astype(BF16), ffn_w_up.astype(BF16), ffn_w_down.astype(BF16)
    for i in range(depth):
        kind, j = i % N_MIXERS, i // N_MIXERS
        if kind == 0:
            h = _pool_layer(h, mod, i, norm_mix_g[i], pool_w[j], pool_scale[j], seq)
        elif kind == 1:
            h = _conv_layer(h, mod, i, norm_mix_g[i], conv_w_in[j], conv_w[j], conv_w_out[j], seq)
        else:
            qkv = _qkv_proj(h, mod, i, norm_mix_g[i], sb_w_qkv[j], seq)
            attn = _sb_attention(qkv, batch, seq, d)
            h = _out_proj(attn, h, mod, i, sb_w_o[j], seq)
        h = _ffn_layer(h, mod, i, norm_ffn_g[i], w_gate, w_up, w_down, final_g, seq,
                       final_norm=(i == depth - 1))
    return h.reshape(batch, seq, d)
```

```python
import functools
import math

import jax
import jax.numpy as jnp
from jax import lax
from jax.experimental import pallas as pl
from jax.experimental.pallas import tpu as pltpu

POOL_WINDOWS = (2, 4, 8, 16)
HEAD_DIM = 128
CONV_WIDTH = 3
N_MIXERS = 3
N_MOD = 6
EPS = 1e-6

HALO = 16
SUBLANES = 8
POOL_HALO = 32

ROW_TILE = 512
POOL_ROW_TILE = 1024
PROJ_ROW_TILE = 1024
FFN_ROW_TILE = 1024
FF_TILE = 512
CONV_COL_TILE = 1024
CONV_SUB_TILE = 512
QKV_COL_TILE = 2048
OUT_ROW_TILE = 512
OUT_COL_TILE = 2048
MOD_COL_TILE = 1024
ACC_COL_TILE = 512
ATTN_TILE = 256
ATTN_CHAINS = 16
SB_LOGIT_LEAD = 3
SB_SCAN_LEAD = 2

SB_DEAD_LOG2 = 151.0
SB_PARKED_LOG2 = 1e30

V7X_VMEM_BYTES = 64 * 1024 * 1024
VMEM_LIMIT_BYTES = 7 * V7X_VMEM_BYTES // 8

F32 = jnp.float32
BF16 = jnp.bfloat16
LOG2E = math.log2(math.e)


def _tile(n, pref, align):
    if n <= pref:
        return n
    t = (pref // align) * align
    while t >= align:
        if n % t == 0:
            return t
        t -= align
    raise ValueError(f"no {align}-aligned tile <= {pref} divides {n}")


def _params(semantics):
    return pltpu.CompilerParams(dimension_semantics=semantics, vmem_limit_bytes=VMEM_LIMIT_BYTES)


def _rmsnorm(x, g):
    ms = jnp.mean(x * x, axis=-1, keepdims=True)
    return x * lax.rsqrt(ms + EPS) * g


def _norm_mod(x, g, shift, scale):
    return _rmsnorm(x, g) * (1.0 + scale) + shift


def _norm_mod_ref(x_ref, g, shift, scale):
    x = x_ref[...]
    inv = lax.rsqrt(jnp.mean(x * x, axis=-1, keepdims=True) + EPS)
    return x_ref[...] * inv * (g * (1.0 + scale)) + shift


def _dot(a, b):
    return jnp.dot(a, b, preferred_element_type=F32)


def _accumulate_dot(o_ref, base_ref, gate, a, w_ref):
    n = o_ref.shape[1]
    tn = _tile(n, ACC_COL_TILE, 128)
    for c0 in range(0, n, tn):
        cols = slice(c0, c0 + tn)
        o_ref[:, cols] = base_ref[:, cols] + gate[:, cols] * _dot(a, w_ref[:, cols])


def _onto_residual_then_output(step, tile_fn, h_ref, o_ref):
    pl.when(step < 1)(lambda: tile_fn(h_ref))
    pl.when(step >= 1)(lambda: tile_fn(o_ref))


def _mod_kernel(c_ref, w_ref, b_ref, o_ref):
    c = c_ref[...]
    sc = c / (1.0 + jnp.exp(-c))
    o_ref[0] = _dot(sc.astype(BF16), w_ref[0].astype(BF16)) + b_ref[0]


def _modulation(c, w_mod, b_mod):
    depth, d, e = w_mod.shape
    b = c.shape[0]
    tn = _tile(e, MOD_COL_TILE, 128)
    return pl.pallas_call(
        _mod_kernel,
        grid=(depth, e // tn),
        in_specs=[
            pl.BlockSpec((b, d), lambda l, j: (0, 0)),
            pl.BlockSpec((1, d, tn), lambda l, j: (l, 0, j)),
            pl.BlockSpec((1, 1, tn), lambda l, j: (l, 0, j)),
        ],
        out_specs=pl.BlockSpec((1, b, tn), lambda l, j: (l, 0, j)),
        out_shape=jax.ShapeDtypeStruct((depth, b, e), F32),
        compiler_params=_params(("parallel", "parallel")),
        name="adaln_mod",
    )(c, w_mod, b_mod.reshape(depth, 1, e))


def _pool_kernel(h_ref, halo_ref, mod_ref, g_ref, w_ref, ps_ref, o_ref, u_ref, a_ref, b_ref, *, tm, seq, windows):
    i = pl.program_id(0)
    pos0 = (i * tm) % seq
    shift, scale, gate = mod_ref[0:1, :], mod_ref[1:2, :], mod_ref[2:3, :]
    g = g_ref[...]
    u_halo = _norm_mod(halo_ref[...], g, shift, scale)
    u_ref[0:POOL_HALO, :] = jnp.where(pos0 == 0, 0.0, u_halo)
    u_ref[POOL_HALO:, :] = _norm_mod_ref(h_ref, g, shift, scale)
    pos = pos0 + lax.broadcasted_iota(jnp.int32, (tm, 1), 0)
    c = w_ref.shape[1]
    for grp, w in enumerate(windows):
        cols = slice(grp * c, (grp + 1) * c)
        src, src_cols, span, lo = u_ref, cols, 1, 0
        for dst in (a_ref, b_ref, a_ref, b_ref):
            if span >= w:
                break
            lo += SUBLANES
            n = tm + POOL_HALO - lo
            dst[pl.ds(lo, n), :] = src[pl.ds(lo, n), src_cols] + src[pl.ds(lo - span, n), src_cols]
            src, src_cols, span = dst, slice(None), 2 * span
        cur = u_ref[pl.ds(POOL_HALO, tm), cols]
        count = jnp.minimum(pos + 1, w).astype(F32)
        diff = src[pl.ds(POOL_HALO, tm), src_cols] / count - cur
        y = _dot(diff.astype(BF16), w_ref[grp]) * ps_ref[:, cols]
        o_ref[:, cols] = h_ref[:, cols] + gate[:, cols] * y


def _pool_layer(h, mod, layer, gain, w_pool, pool_scale, seq):
    t, d = h.shape
    groups, c, _ = w_pool.shape
    passes = max(POOL_WINDOWS).bit_length() - 1
    assert groups == len(POOL_WINDOWS) and groups * c == d
    assert all(w & (w - 1) == 0 for w in POOL_WINDOWS) and passes <= 4 and passes * SUBLANES <= POOL_HALO
    tm = _tile(seq, POOL_ROW_TILE, POOL_HALO)
    halo_blocks = tm // POOL_HALO
    kern = functools.partial(_pool_kernel, tm=tm, seq=seq, windows=POOL_WINDOWS)
    rows = tm + POOL_HALO
    return pl.pallas_call(
        kern,
        grid=(t // tm,),
        in_specs=[
            pl.BlockSpec((tm, d), lambda i: (i, 0)),
            pl.BlockSpec((POOL_HALO, d), lambda i: (jnp.maximum(i * halo_blocks - 1, 0), 0)),
            pl.BlockSpec((None, None, N_MOD, d), lambda i: (layer, (i * tm) // seq, 0, 0)),
            pl.BlockSpec((1, d), lambda i: (0, 0)),
            pl.BlockSpec((groups, c, c), lambda i: (0, 0, 0)),
            pl.BlockSpec((1, d), lambda i: (0, 0)),
        ],
        out_specs=pl.BlockSpec((tm, d), lambda i: (i, 0)),
        out_shape=jax.ShapeDtypeStruct((t, d), F32),
        scratch_shapes=[pltpu.VMEM((rows, d), F32), pltpu.VMEM((rows, c), F32), pltpu.VMEM((rows, c), F32)],
        compiler_params=_params(("parallel",)),
        name=f"pool_mixer_{layer}",
    )(h, h, mod, gain.reshape(1, d), w_pool.astype(BF16), pool_scale.reshape(1, d))


def _conv_kernel(h_ref, halo_ref, mod_ref, g_ref, wb_ref, wc_ref, wv_ref, cw_ref, wo_ref, o_ref,
                 u_ref, z_ref, *, tm, seq):
    i, j = pl.program_id(0), pl.program_id(1)

    @pl.when(j == 0)
    def _():
        pos0 = (i * tm) % seq
        shift, scale = mod_ref[0:1, :], mod_ref[1:2, :]
        g = g_ref[...]
        u_halo = _norm_mod(halo_ref[...], g, shift, scale)
        u_ref[0:HALO, :] = jnp.where(pos0 == 0, 0.0, u_halo).astype(BF16)
        u_ref[HALO:, :] = _norm_mod_ref(h_ref, g, shift, scale).astype(BF16)

    def channel_tile(base_ref):
        u_all = u_ref[...]
        tn = z_ref.shape[1]
        sub = _tile(tn, CONV_SUB_TILE, 128)
        gated = []
        for c0 in range(0, tn, sub):
            cols = slice(c0, c0 + sub)
            z_ref[:, cols] = _dot(u_all, wc_ref[:, cols]) * _dot(u_all, wv_ref[:, cols])
            b_gate = _dot(u_ref[pl.ds(HALO, tm), :], wb_ref[:, cols])
            zc = cw_ref[0:1, cols] * z_ref[pl.ds(HALO - 2, tm), cols]
            zc = zc + cw_ref[1:2, cols] * z_ref[pl.ds(HALO - 1, tm), cols]
            zc = zc + cw_ref[2:3, cols] * z_ref[pl.ds(HALO, tm), cols]
            gated.append((b_gate * zc).astype(BF16))
        _accumulate_dot(o_ref, base_ref, mod_ref[2:3, :], jnp.concatenate(gated, axis=1), wo_ref)

    _onto_residual_then_output(j, channel_tile, h_ref, o_ref)


def _conv_layer(h, mod, layer, gain, w_in, conv_w, w_out, seq):
    t, d = h.shape
    assert conv_w.shape == (CONV_WIDTH, 1, d) and CONV_WIDTH - 1 <= HALO
    tm = _tile(seq, ROW_TILE, HALO)
    tn = _tile(d, CONV_COL_TILE, 128)
    nj = d // tn
    halo_blocks = tm // HALO
    w_in = w_in.astype(BF16)
    kern = functools.partial(_conv_kernel, tm=tm, seq=seq)
    return pl.pallas_call(
        kern,
        grid=(t // tm, nj),
        in_specs=[
            pl.BlockSpec((tm, d), lambda i, j: (i, 0)),
            pl.BlockSpec((HALO, d), lambda i, j: (jnp.maximum(i * halo_blocks - 1, 0), 0)),
            pl.BlockSpec((None, None, N_MOD, d), lambda i, j: (layer, (i * tm) // seq, 0, 0)),
            pl.BlockSpec((1, d), lambda i, j: (0, 0)),
            pl.BlockSpec((d, tn), lambda i, j: (0, j)),
            pl.BlockSpec((d, tn), lambda i, j: (0, nj + j)),
            pl.BlockSpec((d, tn), lambda i, j: (0, 2 * nj + j)),
            pl.BlockSpec((CONV_WIDTH, tn), lambda i, j: (0, j)),
            pl.BlockSpec((tn, d), lambda i, j: (j, 0)),
        ],
        out_specs=pl.BlockSpec((tm, d), lambda i, j: (i, 0)),
        out_shape=jax.ShapeDtypeStruct((t, d), F32),
        scratch_shapes=[pltpu.VMEM((tm + HALO, d), BF16), pltpu.VMEM((tm + HALO, tn), F32)],
        compiler_params=_params(("parallel", "arbitrary")),
        name=f"conv_mixer_{layer}",
    )(h, h, mod, gain.reshape(1, d), w_in, w_in, w_in, conv_w.reshape(CONV_WIDTH, d), w_out.astype(BF16))


def _qkv_kernel(h_ref, mod_ref, g_ref, w_ref, o_ref, u_ref, *, q_tiles, q_scale):
    j = pl.program_id(1)

    @pl.when(j == 0)
    def _():
        u_ref[...] = _norm_mod_ref(h_ref, g_ref[...], mod_ref[0:1, :], mod_ref[1:2, :]).astype(BF16)

    scale = jnp.where(j < q_tiles, q_scale, 1.0).astype(F32)
    u = u_ref[...]
    n = o_ref.shape[1]
    tn = _tile(n, ACC_COL_TILE, 128)
    for c0 in range(0, n, tn):
        o_ref[:, c0:c0 + tn] = (_dot(u, w_ref[:, c0:c0 + tn]) * scale).astype(BF16)


def _qkv_proj(h, mod, layer, gain, w_qkv, seq):
    t, d = h.shape
    e = w_qkv.shape[1]
    tm = _tile(seq, PROJ_ROW_TILE, HALO)
    tn = _tile(d, QKV_COL_TILE, 128)
    kern = functools.partial(_qkv_kernel, q_tiles=d // tn, q_scale=HEAD_DIM ** -0.5 * LOG2E)
    return pl.pallas_call(
        kern,
        grid=(t // tm, e // tn),
        in_specs=[
            pl.BlockSpec((tm, d), lambda i, j: (i, 0)),
            pl.BlockSpec((None, None, N_MOD, d), lambda i, j: (layer, (i * tm) // seq, 0, 0)),
            pl.BlockSpec((1, d), lambda i, j: (0, 0)),
            pl.BlockSpec((d, tn), lambda i, j: (0, j)),
        ],
        out_specs=pl.BlockSpec((tm, tn), lambda i, j: (i, j)),
        out_shape=jax.ShapeDtypeStruct((t, e), BF16),
        scratch_shapes=[pltpu.VMEM((tm, d), BF16)],
        compiler_params=_params(("parallel", "arbitrary")),
        name=f"sb_qkv_{layer}",
    )(h, mod, gain.reshape(1, d), w_qkv.astype(BF16))


def _sb_tiles(q_ref, k_ref, v_ref, jobs, carries, tri, causal_mask, ts):
    n = len(jobs)
    starts = [pl.multiple_of(jnp.maximum(kb, 0) * ts, ts) for _, kb, _ in jobs]
    zs, row_sums, sums = [None] * n, [None] * n, [None] * n
    carries = list(carries)
    pvs = [None] * len(carries)

    def logits(j):
        r = jobs[j][0]
        zs[j] = lax.dot_general(q_ref[pl.ds(r * ts, ts), :], k_ref[pl.ds(starts[j], ts), :],
                                (((1,), (1,)), ((), ())), preferred_element_type=F32)

    def scan(j):
        z, diagonal = zs[j], jobs[j][2]
        neg_abs = pltpu.bitcast(pltpu.bitcast(z, jnp.uint32) | jnp.uint32(1 << 31), F32)
        decay = jnp.maximum(z, 0.0) + jnp.log2(1.0 + jnp.exp2(neg_abs))
        if diagonal:
            decay = jnp.where(causal_mask, decay, 0.0)
        hi = decay.astype(BF16)
        lo = (decay - hi.astype(F32)).astype(BF16)
        sums[j] = _dot(jnp.concatenate([hi, lo], axis=1), tri)
        row_sums[j] = jnp.sum(decay, axis=1, keepdims=True)

    def weights(j):
        r, kb, diagonal = jobs[j]
        carry = jnp.where(kb >= 0, carries[r], SB_PARKED_LOG2)
        a = jnp.exp2(zs[j] - sums[j] - carry)
        if diagonal:
            a = jnp.where(causal_mask, a, 0.0)
        pv = _dot(a.astype(BF16), v_ref[pl.ds(starts[j], ts), :])
        pvs[r] = pv if pvs[r] is None else pvs[r] + pv
        carries[r] = carry + row_sums[j]
        zs[j] = None

    for step in range(n + SB_LOGIT_LEAD + SB_SCAN_LEAD):
        if step < n:
            logits(step)
        if 0 <= step - SB_LOGIT_LEAD < n:
            scan(step - SB_LOGIT_LEAD)
        if 0 <= step - SB_LOGIT_LEAD - SB_SCAN_LEAD < n:
            weights(step - SB_LOGIT_LEAD - SB_SCAN_LEAD)
    return carries, pvs


def _sb_attn_kernel(q_ref, k_ref, v_ref, o_ref, acc_ref, *, ts, chains):
    base = pl.program_id(2) * chains
    row = lax.broadcasted_iota(jnp.int32, (ts, ts), 0)
    col = lax.broadcasted_iota(jnp.int32, (ts, ts), 1)
    tri = jnp.tile((row >= col).astype(BF16), (2, 1))

    def alive(carries):
        return functools.reduce(jnp.minimum, [jnp.min(cr) for cr in carries]) < SB_DEAD_LOG2

    jobs = [(r, base + r, True) for r in range(chains)] + [(r, base + r - 1, False) for r in range(chains)]
    carries, pvs = _sb_tiles(q_ref, k_ref, v_ref, jobs, [jnp.zeros((ts, 1), F32)] * chains, tri, col < row, ts)
    for r in range(chains):
        acc_ref[pl.ds(r * ts, ts), :] = pvs[r]

    def cond(state):
        it, live = state[0], state[1]
        return jnp.logical_and(it < base + chains, live)

    def body(state):
        it, carries = state[0], state[2:]
        jobs = [(r, base + r - it, False) for r in range(chains)]
        carries, pvs = _sb_tiles(q_ref, k_ref, v_ref, jobs, carries, tri, None, ts)
        for r in range(chains):
            acc_ref[pl.ds(r * ts, ts), :] += pvs[r]
        return (it + 1, alive(carries), *carries)

    lax.while_loop(cond, body, (jnp.int32(2), alive(carries), *carries))
    o_ref[...] = acc_ref[...].astype(BF16)


def _sb_attention(qkv, batch, seq, d):
    t = qkv.shape[0]
    heads = d // HEAD_DIM
    ts = _tile(seq, ATTN_TILE, 128)
    chains = math.gcd(seq // ts, ATTN_CHAINS)
    tq = ts * chains
    nq = seq // tq
    kern = functools.partial(_sb_attn_kernel, ts=ts, chains=chains)
    return pl.pallas_call(
        kern,
        grid=(batch, heads, nq),
        in_specs=[
            pl.BlockSpec((tq, HEAD_DIM), lambda b, hd, qi: (b * nq + qi, hd)),
            pl.BlockSpec((seq, HEAD_DIM), lambda b, hd, qi: (b, heads + hd)),
            pl.BlockSpec((seq, HEAD_DIM), lambda b, hd, qi: (b, 2 * heads + hd)),
        ],
        out_specs=pl.BlockSpec((tq, HEAD_DIM), lambda b, hd, qi: (b * nq + qi, hd)),
        out_shape=jax.ShapeDtypeStruct((t, d), BF16),
        scratch_shapes=[pltpu.VMEM((tq, HEAD_DIM), F32)],
        compiler_params=_params(("parallel", "parallel", "arbitrary")),
        name="sb_attention",
    )(qkv, qkv, qkv)


def _oproj_kernel(a_ref, w_ref, h_ref, mod_ref, o_ref):
    _accumulate_dot(o_ref, h_ref, mod_ref[2:3, :], a_ref[...], w_ref)


def _out_proj(attn, h, mod, layer, w_o, seq):
    t, d = h.shape
    tm = _tile(seq, OUT_ROW_TILE, HALO)
    tn = _tile(d, OUT_COL_TILE, 128)
    return pl.pallas_call(
        _oproj_kernel,
        grid=(t // tm, d // tn),
        in_specs=[
            pl.BlockSpec((tm, d), lambda i, j: (i, 0)),
            pl.BlockSpec((d, tn), lambda i, j: (0, j)),
            pl.BlockSpec((tm, tn), lambda i, j: (i, j)),
            pl.BlockSpec((None, None, N_MOD, tn), lambda i, j: (layer, (i * tm) // seq, 0, j)),
        ],
        out_specs=pl.BlockSpec((tm, tn), lambda i, j: (i, j)),
        out_shape=jax.ShapeDtypeStruct((t, d), F32),
        compiler_params=_params(("parallel", "arbitrary")),
        name=f"sb_out_proj_{layer}",
    )(attn, w_o.astype(BF16), h, mod)


def _ffn_kernel(h_hbm, mod_ref, g_ref, wg_ref, wu_ref, wd_ref, fg_ref, *rest, final_norm, cast_next):
    if cast_next:
        next_f32, (o_ref, *next_bf16), (u_ref, h_ref, h_sem) = rest[:3], rest[3:7], rest[7:]
    else:
        next_f32, next_bf16, (o_ref, u_ref, h_ref, h_sem) = (), (), rest
    i, f = pl.program_id(0), pl.program_id(1)
    nb, nf = pl.num_programs(0), pl.num_programs(1)
    last = nf - 1
    tm = h_ref.shape[0]

    def h_copy(block):
        return pltpu.make_async_copy(h_hbm.at[pl.ds(pl.multiple_of(block * tm, tm), tm), :], h_ref, h_sem)

    @pl.when(f == 0)
    def _():
        @pl.when(i == 0)
        def _():
            h_copy(0).start()

        h_copy(i).wait()
        u_ref[...] = _norm_mod_ref(h_ref, g_ref[...], mod_ref[3:4, :], mod_ref[4:5, :]).astype(BF16)

    def hidden_tile(base_ref):
        u = u_ref[...]
        gate = _dot(u, wg_ref[...])
        up = _dot(u, wu_ref[...])
        act = (gate / (1.0 + jnp.exp(-gate)) * up).astype(BF16)
        _accumulate_dot(o_ref, base_ref, mod_ref[5:6, :], act, wd_ref)
        for src_ref, dst_ref in zip(next_f32, next_bf16):
            dst_ref[...] = src_ref[...].astype(BF16)

    _onto_residual_then_output(f, hidden_tile, h_ref, o_ref)

    refill = jnp.minimum(1, last)

    @pl.when(jnp.logical_and(f == refill, i + 1 < nb))
    def _():
        h_copy(i + 1).start()

    if final_norm:
        @pl.when(f == last)
        def _():
            o_ref[...] = _rmsnorm(o_ref[...], fg_ref[...])


def _ffn_layer(h, mod, layer, gain, w_gate, w_up, w_down, final_g, seq, final_norm, next_f32=None):
    t, d = h.shape
    ff = w_gate.shape[1]
    tm = _tile(seq, FFN_ROW_TILE, HALO)
    tf = _tile(ff, FF_TILE, 128)
    nb, nf = t // tm, ff // tf
    in_specs = [
        pl.BlockSpec(memory_space=pl.ANY),
        pl.BlockSpec((None, None, N_MOD, d), lambda i, f: (layer, (i * tm) // seq, 0, 0)),
        pl.BlockSpec((1, d), lambda i, f: (0, 0)),
        pl.BlockSpec((d, tf), lambda i, f: (0, f)),
        pl.BlockSpec((d, tf), lambda i, f: (0, f)),
        pl.BlockSpec((tf, d), lambda i, f: (f, 0)),
        pl.BlockSpec((1, d), lambda i, f: (0, 0)),
    ]
    out_specs = [pl.BlockSpec((tm, d), lambda i, f: (i, 0))]
    out_shape = [jax.ShapeDtypeStruct((t, d), F32)]
    args = [h, mod, gain.reshape(1, d), w_gate, w_up, w_down, final_g.reshape(1, d)]
    cast_next = next_f32 is not None
    if cast_next:
        rg, rd = d // nb, ff // (nb * nf)
        assert rg * nb == d and rd * nb * nf == ff and rg % HALO == 0 and rd % HALO == 0
        in_specs += [
            pl.BlockSpec((None, rg, tf), lambda i, f: (layer + 1, i, f)),
            pl.BlockSpec((None, rg, tf), lambda i, f: (layer + 1, i, f)),
            pl.BlockSpec((None, rd, d), lambda i, f: (layer + 1, i * nf + f, 0)),
        ]
        out_specs += [
            pl.BlockSpec((rg, tf), lambda i, f: (i, f)),
            pl.BlockSpec((rg, tf), lambda i, f: (i, f)),
            pl.BlockSpec((rd, d), lambda i, f: (i * nf + f, 0)),
        ]
        out_shape += [jax.ShapeDtypeStruct((d, ff), BF16), jax.ShapeDtypeStruct((d, ff), BF16),
                      jax.ShapeDtypeStruct((ff, d), BF16)]
        args += list(next_f32)
    kern = functools.partial(_ffn_kernel, final_norm=final_norm, cast_next=cast_next)
    out = pl.pallas_call(
        kern,
        grid=(nb, nf),
        in_specs=in_specs,
        out_specs=out_specs,
        out_shape=out_shape,
        scratch_shapes=[pltpu.VMEM((tm, d), BF16), pltpu.VMEM((tm, d), F32), pltpu.SemaphoreType.DMA(())],
        compiler_params=_params(("arbitrary", "arbitrary")),
        name=f"ffn_{layer}",
    )(*args)
    return out[0], tuple(out[1:])


def kernel(x, c, norm_mix_g, norm_ffn_g, w_mod, b_mod, pool_w, pool_scale, conv_w_in, conv_w,
           conv_w_out, sb_w_qkv, sb_w_o, ffn_w_gate, ffn_w_up, ffn_w_down, final_g):
    batch, seq, d = x.shape
    depth = w_mod.shape[0]
    assert w_mod.shape == (depth, d, N_MOD * d) and d % HEAD_DIM == 0
    mod = _modulation(c, w_mod, b_mod).reshape(depth, batch, N_MOD, d)
    h = x.reshape(batch * seq, d)
    weights = (ffn_w_gate[0].astype(BF16), ffn_w_up[0].astype(BF16), ffn_w_down[0].astype(BF16))
    next_f32 = (ffn_w_gate, ffn_w_up, ffn_w_down)
    for i in range(depth):
        kind, j = i % N_MIXERS, i // N_MIXERS
        if kind == 0:
            h = _pool_layer(h, mod, i, norm_mix_g[i], pool_w[j], pool_scale[j], seq)
        elif kind == 1:
            h = _conv_layer(h, mod, i, norm_mix_g[i], conv_w_in[j], conv_w[j], conv_w_out[j], seq)
        else:
            qkv = _qkv_proj(h, mod, i, norm_mix_g[i], sb_w_qkv[j], seq)
            attn = _sb_attention(qkv, batch, seq, d)
            h = _out_proj(attn, h, mod, i, sb_w_o[j], seq)
        is_last = i == depth - 1
        h, weights = _ffn_layer(h, mod, i, norm_ffn_g[i], *weights, final_g, seq, final_norm=is_last,
                                next_f32=None if is_last else next_f32)
    return h.reshape(batch, seq, d)
```

```python
import functools
import math

import jax
import jax.numpy as jnp
from jax import lax
from jax.experimental import pallas as pl
from jax.experimental.pallas import tpu as pltpu

POOL_WINDOWS = (2, 4, 8, 16)
HEAD_DIM = 128
CONV_WIDTH = 3
N_MIXERS = 3
N_MOD = 6
EPS = 1e-6

HALO = 16
SUBLANES = 8
POOL_HALO = 32

ROW_TILE = 512
POOL_ROW_TILE = 1024
PROJ_ROW_TILE = 1024
FFN_ROW_TILE = 1024
FF_TILE = 512
CONV_COL_TILE = 1024
CONV_SUB_TILE = 512
QKV_COL_TILE = 2048
OUT_ROW_TILE = 512
OUT_COL_TILE = 2048
MOD_COL_TILE = 1024
ACC_COL_TILE = 512
ATTN_TILE = 256
ATTN_CHAINS = 16
SB_LOGIT_LEAD = 3
SB_SCAN_LEAD = 2

SB_DEAD_LOG2 = 151.0
SB_PARKED_LOG2 = 1e30

V7X_VMEM_BYTES = 64 * 1024 * 1024
VMEM_LIMIT_BYTES = 7 * V7X_VMEM_BYTES // 8

F32 = jnp.float32
BF16 = jnp.bfloat16
LOG2E = math.log2(math.e)


def _tile(n, pref, align):
    if n <= pref:
        return n
    t = (pref // align) * align
    while t >= align:
        if n % t == 0:
            return t
        t -= align
    raise ValueError(f"no {align}-aligned tile <= {pref} divides {n}")


def _params(semantics):
    return pltpu.CompilerParams(dimension_semantics=semantics, vmem_limit_bytes=VMEM_LIMIT_BYTES)


def _rmsnorm(x, g):
    ms = jnp.mean(x * x, axis=-1, keepdims=True)
    return x * lax.rsqrt(ms + EPS) * g


def _norm_mod(x, g, shift, scale):
    return _rmsnorm(x, g) * (1.0 + scale) + shift


def _norm_mod_ref(x_ref, g, shift, scale):
    x = x_ref[...]
    inv = lax.rsqrt(jnp.mean(x * x, axis=-1, keepdims=True) + EPS)
    return x_ref[...] * inv * (g * (1.0 + scale)) + shift


def _dot(a, b):
    return jnp.dot(a, b, preferred_element_type=F32)


def _accumulate_dot(o_ref, base_ref, gate, a, w_ref):
    n = o_ref.shape[1]
    tn = _tile(n, ACC_COL_TILE, 128)
    for c0 in range(0, n, tn):
        cols = slice(c0, c0 + tn)
        o_ref[:, cols] = base_ref[:, cols] + gate[:, cols] * _dot(a, w_ref[:, cols])


def _onto_residual_then_output(step, tile_fn, h_ref, o_ref):
    pl.when(step < 1)(lambda: tile_fn(h_ref))
    pl.when(step >= 1)(lambda: tile_fn(o_ref))


def _mod_kernel(c_ref, w_ref, b_ref, o_ref):
    c = c_ref[...]
    sc = c / (1.0 + jnp.exp(-c))
    o_ref[0] = _dot(sc.astype(BF16), w_ref[0].astype(BF16)) + b_ref[0]


def _modulation(c, w_mod, b_mod):
    depth, d, e = w_mod.shape
    b = c.shape[0]
    tn = _tile(e, MOD_COL_TILE, 128)
    return pl.pallas_call(
        _mod_kernel,
        grid=(depth, e // tn),
        in_specs=[
            pl.BlockSpec((b, d), lambda l, j: (0, 0)),
            pl.BlockSpec((1, d, tn), lambda l, j: (l, 0, j)),
            pl.BlockSpec((1, 1, tn), lambda l, j: (l, 0, j)),
        ],
        out_specs=pl.BlockSpec((1, b, tn), lambda l, j: (l, 0, j)),
        out_shape=jax.ShapeDtypeStruct((depth, b, e), F32),
        compiler_params=_params(("parallel", "parallel")),
        name="adaln_mod",
    )(c, w_mod, b_mod.reshape(depth, 1, e))


def _pool_kernel(h_ref, halo_ref, mod_ref, g_ref, w_ref, ps_ref, o_ref, u_ref, a_ref, b_ref, *, tm, seq, windows):
    i = pl.program_id(0)
    pos0 = (i * tm) % seq
    shift, scale, gate = mod_ref[0:1, :], mod_ref[1:2, :], mod_ref[2:3, :]
    g = g_ref[...]
    u_halo = _norm_mod(halo_ref[...], g, shift, scale)
    u_ref[0:POOL_HALO, :] = jnp.where(pos0 == 0, 0.0, u_halo)
    u_ref[POOL_HALO:, :] = _norm_mod_ref(h_ref, g, shift, scale)
    pos = pos0 + lax.broadcasted_iota(jnp.int32, (tm, 1), 0)
    c = w_ref.shape[1]
    for grp, w in enumerate(windows):
        cols = slice(grp * c, (grp + 1) * c)
        src, src_cols, span, lo = u_ref, cols, 1, 0
        for dst in (a_ref, b_ref, a_ref, b_ref):
            if span >= w:
                break
            lo += SUBLANES
            n = tm + POOL_HALO - lo
            dst[pl.ds(lo, n), :] = src[pl.ds(lo, n), src_cols] + src[pl.ds(lo - span, n), src_cols]
            src, src_cols, span = dst, slice(None), 2 * span
        cur = u_ref[pl.ds(POOL_HALO, tm), cols]
        count = jnp.minimum(pos + 1, w).astype(F32)
        diff = src[pl.ds(POOL_HALO, tm), src_cols] / count - cur
        y = _dot(diff.astype(BF16), w_ref[grp]) * ps_ref[:, cols]
        o_ref[:, cols] = h_ref[:, cols] + gate[:, cols] * y


def _pool_layer(h, mod, layer, gain, w_pool, pool_scale, seq):
    t, d = h.shape
    groups, c, _ = w_pool.shape
    passes = max(POOL_WINDOWS).bit_length() - 1
    assert groups == len(POOL_WINDOWS) and groups * c == d
    assert all(w & (w - 1) == 0 for w in POOL_WINDOWS) and passes <= 4 and passes * SUBLANES <= POOL_HALO
    tm = _tile(seq, POOL_ROW_TILE, POOL_HALO)
    halo_blocks = tm // POOL_HALO
    kern = functools.partial(_pool_kernel, tm=tm, seq=seq, windows=POOL_WINDOWS)
    rows = tm + POOL_HALO
    return pl.pallas_call(
        kern,
        grid=(t // tm,),
        in_specs=[
            pl.BlockSpec((tm, d), lambda i: (i, 0)),
            pl.BlockSpec((POOL_HALO, d), lambda i: (jnp.maximum(i * halo_blocks - 1, 0), 0)),
            pl.BlockSpec((None, None, N_MOD, d), lambda i: (layer, (i * tm) // seq, 0, 0)),
            pl.BlockSpec((1, d), lambda i: (0, 0)),
            pl.BlockSpec((groups, c, c), lambda i: (0, 0, 0)),
            pl.BlockSpec((1, d), lambda i: (0, 0)),
        ],
        out_specs=pl.BlockSpec((tm, d), lambda i: (i, 0)),
        out_shape=jax.ShapeDtypeStruct((t, d), F32),
        scratch_shapes=[pltpu.VMEM((rows, d), F32), pltpu.VMEM((rows, c), F32), pltpu.VMEM((rows, c), F32)],
        compiler_params=_params(("parallel",)),
        name=f"pool_mixer_{layer}",
    )(h, h, mod, gain.reshape(1, d), w_pool.astype(BF16), pool_scale.reshape(1, d))


def _conv_kernel(h_ref, halo_ref, mod_ref, g_ref, wb_ref, wc_ref, wv_ref, cw_ref, wo_ref, o_ref,
                 u_ref, z_ref, *, tm, seq):
    i, j = pl.program_id(0), pl.program_id(1)

    @pl.when(j == 0)
    def _():
        pos0 = (i * tm) % seq
        shift, scale = mod_ref[0:1, :], mod_ref[1:2, :]
        g = g_ref[...]
        u_halo = _norm_mod(halo_ref[...], g, shift, scale)
        u_ref[0:HALO, :] = jnp.where(pos0 == 0, 0.0, u_halo).astype(BF16)
        u_ref[HALO:, :] = _norm_mod_ref(h_ref, g, shift, scale).astype(BF16)

    def channel_tile(base_ref):
        u_all = u_ref[...]
        tn = z_ref.shape[1]
        sub = _tile(tn, CONV_SUB_TILE, 128)
        gated = []
        for c0 in range(0, tn, sub):
            cols = slice(c0, c0 + sub)
            z_ref[:, cols] = _dot(u_all, wc_ref[:, cols]) * _dot(u_all, wv_ref[:, cols])
            b_gate = _dot(u_ref[pl.ds(HALO, tm), :], wb_ref[:, cols])
            zc = cw_ref[0:1, cols] * z_ref[pl.ds(HALO - 2, tm), cols]
            zc = zc + cw_ref[1:2, cols] * z_ref[pl.ds(HALO - 1, tm), cols]
            zc = zc + cw_ref[2:3, cols] * z_ref[pl.ds(HALO, tm), cols]
            gated.append((b_gate * zc).astype(BF16))
        _accumulate_dot(o_ref, base_ref, mod_ref[2:3, :], jnp.concatenate(gated, axis=1), wo_ref)

    _onto_residual_then_output(j, channel_tile, h_ref, o_ref)


def _conv_layer(h, mod, layer, gain, w_in, conv_w, w_out, seq):
    t, d = h.shape
    assert conv_w.shape == (CONV_WIDTH, 1, d) and CONV_WIDTH - 1 <= HALO
    tm = _tile(seq, ROW_TILE, HALO)
    tn = _tile(d, CONV_COL_TILE, 128)
    nj = d // tn
    halo_blocks = tm // HALO
    w_in = w_in.astype(BF16)
    kern = functools.partial(_conv_kernel, tm=tm, seq=seq)
    return pl.pallas_call(
        kern,
        grid=(t // tm, nj),
        in_specs=[
            pl.BlockSpec((tm, d), lambda i, j: (i, 0)),
            pl.BlockSpec((HALO, d), lambda i, j: (jnp.maximum(i * halo_blocks - 1, 0), 0)),
            pl.BlockSpec((None, None, N_MOD, d), lambda i, j: (layer, (i * tm) // seq, 0, 0)),
            pl.BlockSpec((1, d), lambda i, j: (0, 0)),
            pl.BlockSpec((d, tn), lambda i, j: (0, j)),
            pl.BlockSpec((d, tn), lambda i, j: (0, nj + j)),
            pl.BlockSpec((d, tn), lambda i, j: (0, 2 * nj + j)),
            pl.BlockSpec((CONV_WIDTH, tn), lambda i, j: (0, j)),
            pl.BlockSpec((tn, d), lambda i, j: (j, 0)),
        ],
        out_specs=pl.BlockSpec((tm, d), lambda i, j: (i, 0)),
        out_shape=jax.ShapeDtypeStruct((t, d), F32),
        scratch_shapes=[pltpu.VMEM((tm + HALO, d), BF16), pltpu.VMEM((tm + HALO, tn), F32)],
        compiler_params=_params(("parallel", "arbitrary")),
        name=f"conv_mixer_{layer}",
    )(h, h, mod, gain.reshape(1, d), w_in, w_in, w_in, conv_w.reshape(CONV_WIDTH, d), w_out.astype(BF16))


def _qkv_kernel(h_ref, mod_ref, g_ref, w_ref, o_ref, u_ref, *, q_tiles, q_scale):
    j = pl.program_id(1)

    @pl.when(j == 0)
    def _():
        u_ref[...] = _norm_mod_ref(h_ref, g_ref[...], mod_ref[0:1, :], mod_ref[1:2, :]).astype(BF16)

    scale = jnp.where(j < q_tiles, q_scale, 1.0).astype(F32)
    u = u_ref[...]
    n = o_ref.shape[1]
    tn = _tile(n, ACC_COL_TILE, 128)
    for c0 in range(0, n, tn):
        o_ref[:, c0:c0 + tn] = (_dot(u, w_ref[:, c0:c0 + tn]) * scale).astype(BF16)


def _qkv_proj(h, mod, layer, gain, w_qkv, seq):
    t, d = h.shape
    e = w_qkv.shape[1]
    tm = _tile(seq, PROJ_ROW_TILE, HALO)
    tn = _tile(d, QKV_COL_TILE, 128)
    kern = functools.partial(_qkv_kernel, q_tiles=d // tn, q_scale=HEAD_DIM ** -0.5 * LOG2E)
    return pl.pallas_call(
        kern,
        grid=(t // tm, e // tn),
        in_specs=[
            pl.BlockSpec((tm, d), lambda i, j: (i, 0)),
            pl.BlockSpec((None, None, N_MOD, d), lambda i, j: (layer, (i * tm) // seq, 0, 0)),
            pl.BlockSpec((1, d), lambda i, j: (0, 0)),
            pl.BlockSpec((d, tn), lambda i, j: (0, j)),
        ],
        out_specs=pl.BlockSpec((tm, tn), lambda i, j: (i, j)),
        out_shape=jax.ShapeDtypeStruct((t, e), BF16),
        scratch_shapes=[pltpu.VMEM((tm, d), BF16)],
        compiler_params=_params(("parallel", "arbitrary")),
        name=f"sb_qkv_{layer}",
    )(h, mod, gain.reshape(1, d), w_qkv.astype(BF16))


def _sb_tiles(q_ref, k_ref, v_ref, jobs, carries, tri, causal_mask, ts):
    n = len(jobs)
    starts = [pl.multiple_of(jnp.maximum(kb, 0) * ts, ts) for _, kb, _ in jobs]
    zs, row_sums, sums = [None] * n, [None] * n, [None] * n
    carries = list(carries)
    pvs = [None] * len(carries)

    def logits(j):
        r = jobs[j][0]
        zs[j] = lax.dot_general(q_ref[pl.ds(r * ts, ts), :], k_ref[pl.ds(starts[j], ts), :],
                                (((1,), (1,)), ((), ())), preferred_element_type=F32)

    def scan(j):
        z, diagonal = zs[j], jobs[j][2]
        neg_abs = pltpu.bitcast(pltpu.bitcast(z, jnp.uint32) | jnp.uint32(1 << 31), F32)
        decay = jnp.maximum(z, 0.0) + jnp.log2(1.0 + jnp.exp2(neg_abs))
        if diagonal:
            decay = jnp.where(causal_mask, decay, 0.0)
        hi = decay.astype(BF16)
        lo = (decay - hi.astype(F32)).astype(BF16)
        sums[j] = _dot(jnp.concatenate([hi, lo], axis=1), tri)
        row_sums[j] = jnp.sum(decay, axis=1, keepdims=True)

    def weights(j):
        r, kb, diagonal = jobs[j]
        carry = jnp.where(kb >= 0, carries[r], SB_PARKED_LOG2)
        a = jnp.exp2(zs[j] - sums[j] - carry)
        if diagonal:
            a = jnp.where(causal_mask, a, 0.0)
        pv = _dot(a.astype(BF16), v_ref[pl.ds(starts[j], ts), :])
        pvs[r] = pv if pvs[r] is None else pvs[r] + pv
        carries[r] = carry + row_sums[j]
        zs[j] = None

    for step in range(n + SB_LOGIT_LEAD + SB_SCAN_LEAD):
        if step < n:
            logits(step)
        if 0 <= step - SB_LOGIT_LEAD < n:
            scan(step - SB_LOGIT_LEAD)
        if 0 <= step - SB_LOGIT_LEAD - SB_SCAN_LEAD < n:
            weights(step - SB_LOGIT_LEAD - SB_SCAN_LEAD)
    return carries, pvs


def _sb_attn_kernel(q_ref, k_ref, v_ref, o_ref, acc_ref, *, ts, chains):
    base = pl.program_id(2) * chains
    row = lax.broadcasted_iota(jnp.int32, (ts, ts), 0)
    col = lax.broadcasted_iota(jnp.int32, (ts, ts), 1)
    tri = jnp.tile((row >= col).astype(BF16), (2, 1))

    def alive(carries):
        return functools.reduce(jnp.minimum, [jnp.min(cr) for cr in carries]) < SB_DEAD_LOG2

    jobs = [(r, base + r, True) for r in range(chains)] + [(r, base + r - 1, False) for r in range(chains)]
    carries, pvs = _sb_tiles(q_ref, k_ref, v_ref, jobs, [jnp.zeros((ts, 1), F32)] * chains, tri, col < row, ts)
    for r in range(chains):
        acc_ref[pl.ds(r * ts, ts), :] = pvs[r]

    def cond(state):
        it, live = state[0], state[1]
        return jnp.logical_and(it < base + chains, live)

    def body(state):
        it, carries = state[0], state[2:]
        jobs = [(r, base + r - it, False) for r in range(chains)]
        carries, pvs = _sb_tiles(q_ref, k_ref, v_ref, jobs, carries, tri, None, ts)
        for r in range(chains):
            acc_ref[pl.ds(r * ts, ts), :] += pvs[r]
        return (it + 1, alive(carries), *carries)

    lax.while_loop(cond, body, (jnp.int32(2), alive(carries), *carries))
    o_ref[...] = acc_ref[...].astype(BF16)


def _sb_attention(qkv, batch, seq, d):
    t = qkv.shape[0]
    heads = d // HEAD_DIM
    ts = _tile(seq, ATTN_TILE, 128)
    chains = math.gcd(seq // ts, ATTN_CHAINS)
    tq = ts * chains
    nq = seq // tq
    kern = functools.partial(_sb_attn_kernel, ts=ts, chains=chains)
    return pl.pallas_call(
        kern,
        grid=(batch, heads, nq),
        in_specs=[
            pl.BlockSpec((tq, HEAD_DIM), lambda b, hd, qi: (b * nq + qi, hd)),
            pl.BlockSpec((seq, HEAD_DIM), lambda b, hd, qi: (b, heads + hd)),
            pl.BlockSpec((seq, HEAD_DIM), lambda b, hd, qi: (b, 2 * heads + hd)),
        ],
        out_specs=pl.BlockSpec((tq, HEAD_DIM), lambda b, hd, qi: (b * nq + qi, hd)),
        out_shape=jax.ShapeDtypeStruct((t, d), BF16),
        scratch_shapes=[pltpu.VMEM((tq, HEAD_DIM), F32)],
        compiler_params=_params(("parallel", "parallel", "arbitrary")),
        name="sb_attention",
    )(qkv, qkv, qkv)


def _oproj_kernel(a_ref, w_ref, h_ref, mod_ref, o_ref):
    _accumulate_dot(o_ref, h_ref, mod_ref[2:3, :], a_ref[...], w_ref)


def _out_proj(attn, h, mod, layer, w_o, seq):
    t, d = h.shape
    tm = _tile(seq, OUT_ROW_TILE, HALO)
    tn = _tile(d, OUT_COL_TILE, 128)
    return pl.pallas_call(
        _oproj_kernel,
        grid=(t // tm, d // tn),
        in_specs=[
            pl.BlockSpec((tm, d), lambda i, j: (i, 0)),
            pl.BlockSpec((d, tn), lambda i, j: (0, j)),
            pl.BlockSpec((tm, tn), lambda i, j: (i, j)),
            pl.BlockSpec((None, None, N_MOD, tn), lambda i, j: (layer, (i * tm) // seq, 0, j)),
        ],
        out_specs=pl.BlockSpec((tm, tn), lambda i, j: (i, j)),
        out_shape=jax.ShapeDtypeStruct((t, d), F32),
        compiler_params=_params(("parallel", "arbitrary")),
        name=f"sb_out_proj_{layer}",
    )(attn, w_o.astype(BF16), h, mod)


def _ffn_kernel(h_hbm, mod_ref, g_ref, wg_ref, wu_ref, wd_ref, fg_ref, *rest, final_norm, cast_next):
    n = cast_next
    next_f32, o_ref, next_bf16, (u_ref, h_ref, h_sem) = rest[:n], rest[n], rest[n + 1:2 * n + 1], rest[2 * n + 1:]
    i, f = pl.program_id(0), pl.program_id(1)
    nb, nf = pl.num_programs(0), pl.num_programs(1)
    last = nf - 1
    tm = h_ref.shape[0]

    def h_copy(block):
        return pltpu.make_async_copy(h_hbm.at[pl.ds(pl.multiple_of(block * tm, tm), tm), :], h_ref, h_sem)

    @pl.when(f == 0)
    def _():
        @pl.when(i == 0)
        def _():
            h_copy(0).start()

        h_copy(i).wait()
        u_ref[...] = _norm_mod_ref(h_ref, g_ref[...], mod_ref[3:4, :], mod_ref[4:5, :]).astype(BF16)

    def hidden_tile(base_ref):
        u = u_ref[...]
        gate = _dot(u, wg_ref[...])
        up = _dot(u, wu_ref[...])
        act = (gate / (1.0 + jnp.exp(-gate)) * up).astype(BF16)
        _accumulate_dot(o_ref, base_ref, mod_ref[5:6, :], act, wd_ref)
        for src_ref, dst_ref in zip(next_f32, next_bf16):
            dst_ref[...] = src_ref[...].astype(BF16)

    _onto_residual_then_output(f, hidden_tile, h_ref, o_ref)

    refill = jnp.minimum(1, last)

    @pl.when(jnp.logical_and(f == refill, i + 1 < nb))
    def _():
        h_copy(i + 1).start()

    if final_norm:
        @pl.when(f == last)
        def _():
            o_ref[...] = _rmsnorm(o_ref[...], fg_ref[...])


def _ffn_layer(h, mod, layer, gain, w_gate, w_up, w_down, final_g, seq, final_norm, next_f32=None,
               mixer_f32=()):
    t, d = h.shape
    ff = w_gate.shape[1]
    tm = _tile(seq, FFN_ROW_TILE, HALO)
    tf = _tile(ff, FF_TILE, 128)
    nb, nf = t // tm, ff // tf
    in_specs = [
        pl.BlockSpec(memory_space=pl.ANY),
        pl.BlockSpec((None, None, N_MOD, d), lambda i, f: (layer, (i * tm) // seq, 0, 0)),
        pl.BlockSpec((1, d), lambda i, f: (0, 0)),
        pl.BlockSpec((d, tf), lambda i, f: (0, f)),
        pl.BlockSpec((d, tf), lambda i, f: (0, f)),
        pl.BlockSpec((tf, d), lambda i, f: (f, 0)),
        pl.BlockSpec((1, d), lambda i, f: (0, 0)),
    ]
    out_specs = [pl.BlockSpec((tm, d), lambda i, f: (i, 0))]
    out_shape = [jax.ShapeDtypeStruct((t, d), F32)]
    args = [h, mod, gain.reshape(1, d), w_gate, w_up, w_down, final_g.reshape(1, d)]
    cast_next = next_f32 is not None
    if cast_next:
        rg, rd = d // nb, ff // (nb * nf)
        assert rg * nb == d and rd * nb * nf == ff and rg % HALO == 0 and rd % HALO == 0
        in_specs += [
            pl.BlockSpec((None, rg, tf), lambda i, f: (layer + 1, i, f)),
            pl.BlockSpec((None, rg, tf), lambda i, f: (layer + 1, i, f)),
            pl.BlockSpec((None, rd, d), lambda i, f: (layer + 1, i * nf + f, 0)),
        ]
        out_specs += [
            pl.BlockSpec((rg, tf), lambda i, f: (i, f)),
            pl.BlockSpec((rg, tf), lambda i, f: (i, f)),
            pl.BlockSpec((rd, d), lambda i, f: (i * nf + f, 0)),
        ]
        out_shape += [jax.ShapeDtypeStruct((d, ff), BF16), jax.ShapeDtypeStruct((d, ff), BF16),
                      jax.ShapeDtypeStruct((ff, d), BF16)]
        args += list(next_f32)
    for stack, j in mixer_f32:
        rows, cols = stack.shape[1:]
        rg, col_tiles = rows // nb, 1
        while 2 * col_tiles <= nf and cols % (2 * col_tiles * 128) == 0:
            col_tiles *= 2
        tc = cols // col_tiles
        assert rg * nb == rows and rg % HALO == 0
        in_specs.append(pl.BlockSpec((None, rg, tc), lambda i, f, j=j, m=col_tiles - 1: (j, i, jnp.minimum(f, m))))
        out_specs.append(pl.BlockSpec((rg, tc), lambda i, f, m=col_tiles - 1: (i, jnp.minimum(f, m))))
        out_shape.append(jax.ShapeDtypeStruct((rows, cols), BF16))
        args.append(stack)
    n_cast = len(args) - 7
    kern = functools.partial(_ffn_kernel, final_norm=final_norm, cast_next=n_cast)
    out = pl.pallas_call(
        kern,
        grid=(nb, nf),
        in_specs=in_specs,
        out_specs=out_specs,
        out_shape=out_shape,
        scratch_shapes=[pltpu.VMEM((tm, d), BF16), pltpu.VMEM((tm, d), F32), pltpu.SemaphoreType.DMA(())],
        compiler_params=_params(("arbitrary", "arbitrary")),
        name=f"ffn_{layer}",
    )(*args)
    n_ffn = 3 if cast_next else 0
    return out[0], tuple(out[1:1 + n_ffn]), tuple(out[1 + n_ffn:])


def kernel(x, c, norm_mix_g, norm_ffn_g, w_mod, b_mod, pool_w, pool_scale, conv_w_in, conv_w,
           conv_w_out, sb_w_qkv, sb_w_o, ffn_w_gate, ffn_w_up, ffn_w_down, final_g):
    batch, seq, d = x.shape
    depth = w_mod.shape[0]
    assert w_mod.shape == (depth, d, N_MOD * d) and d % HEAD_DIM == 0
    mod = _modulation(c, w_mod, b_mod).reshape(depth, batch, N_MOD, d)
    h = x.reshape(batch * seq, d)
    weights = (ffn_w_gate[0].astype(BF16), ffn_w_up[0].astype(BF16), ffn_w_down[0].astype(BF16))
    next_f32 = (ffn_w_gate, ffn_w_up, ffn_w_down)
    mixer_stacks = {1: (conv_w_in, conv_w_out), 2: (sb_w_qkv, sb_w_o)}
    mixer_w = ()
    for i in range(depth):
        kind, j = i % N_MIXERS, i // N_MIXERS
        if kind == 0:
            h = _pool_layer(h, mod, i, norm_mix_g[i], pool_w[j], pool_scale[j], seq)
        else:
            w_a, w_b = mixer_w if mixer_w else (s[j] for s in mixer_stacks[kind])
            if kind == 1:
                h = _conv_layer(h, mod, i, norm_mix_g[i], w_a, conv_w[j], w_b, seq)
            else:
                qkv = _qkv_proj(h, mod, i, norm_mix_g[i], w_a, seq)
                attn = _sb_attention(qkv, batch, seq, d)
                h = _out_proj(attn, h, mod, i, w_b, seq)
        is_last = i == depth - 1
        nxt = (i + 1) % N_MIXERS
        to_cast = () if is_last or nxt == 0 else tuple((s, (i + 1) // N_MIXERS) for s in mixer_stacks[nxt])
        h, weights, mixer_w = _ffn_layer(h, mod, i, norm_ffn_g[i], *weights, final_g, seq, final_norm=is_last,
                                         next_f32=None if is_last else next_f32, mixer_f32=to_cast)
    return h.reshape(batch, seq, d)
```
